```python
import math
import jax, jax.numpy as jnp
from jax import lax
import numpy as np

D_MODEL = 2048
BATCH = 1
SEQ = 8192
DEPTH = 1

EXPAND = 2
D_MIX = EXPAND * D_MODEL
W_CONV = D_MIX // 2
W_ATTN = D_MIX - W_CONV
HEAD_DIM = 128
N_HEADS = W_ATTN // HEAD_DIM
CONV_WIDTH = 31
Q_BLOCK = 128
LN_EPS = 1e-5
DN_ALPHA = (2.0 * DEPTH) ** 0.25
DN_BETA = (8.0 * DEPTH) ** -0.25

SPLIT_SIZES = (
    W_CONV,
    W_CONV,
    W_CONV,
    W_ATTN,
    W_ATTN,
    W_ATTN,
    N_HEADS,
    W_ATTN,
)
D_IN = sum(SPLIT_SIZES)

kernel_name = "hybrid_conformer_conv_fox_attn_deepnorm"


def _layer_norm(x, g, b):
    xf = x.astype(jnp.float32)
    mu = jnp.mean(xf, axis=-1, keepdims=True)
    var = jnp.mean(jnp.square(xf - mu), axis=-1, keepdims=True)
    y = (xf - mu) * lax.rsqrt(var + LN_EPS) * g.astype(jnp.float32) + b.astype(jnp.float32)
    return y.astype(x.dtype)


def _split_columns(h):
    offsets = np.cumsum(SPLIT_SIZES)[:-1].tolist()
    return jnp.split(h, offsets, axis=-1)


def _conformer_conv(glu_val, glu_gate, w_dw, b_dw, g_cn, b_cn):
    u = glu_val * jax.nn.sigmoid(glu_gate)
    u_pad = jnp.pad(u, ((0, 0), (CONV_WIDTH - 1, 0), (0, 0)))
    c = lax.conv_general_dilated(
        u_pad, w_dw[:, None, :].astype(u.dtype),
        window_strides=(1,), padding="VALID",
        dimension_numbers=("NWC", "WIO", "NWC"),
        feature_group_count=W_CONV) + b_dw
    c = _layer_norm(c, g_cn, b_cn)
    return jax.nn.silu(c)


def _forgetting_attention(q, k, v, f_logit):
    b, s, _ = q.shape
    to_heads = lambda t: t.reshape(b, s, N_HEADS, HEAD_DIM).transpose(0, 2, 1, 3)
    qh, kh, vh = to_heads(q), to_heads(k), to_heads(v)
    log_f = jax.nn.log_sigmoid(f_logit.astype(jnp.float32)).transpose(0, 2, 1)
    cum = jnp.cumsum(log_f, axis=-1)
    scale = 1.0 / math.sqrt(HEAD_DIM)
    n_blocks = s // Q_BLOCK
    outs = []
    for i in range(n_blocks):
        q0, q1 = i * Q_BLOCK, (i + 1) * Q_BLOCK
        q_blk = qh[:, :, q0:q1]
        k_pre, v_pre = kh[:, :, :q1], vh[:, :, :q1]
        logits = jnp.einsum("bhqd,bhkd->bhqk", q_blk, k_pre).astype(jnp.float32) * scale
        logits = logits + (cum[:, :, q0:q1, None] - cum[:, :, None, :q1])
        q_pos = q0 + jnp.arange(Q_BLOCK)
        k_pos = jnp.arange(q1)
        logits = jnp.where(k_pos[None, :] <= q_pos[:, None], logits, -jnp.inf)
        p = jax.nn.softmax(logits, axis=-1).astype(v_pre.dtype)
        outs.append(jnp.einsum("bhqk,bhkd->bhqd", p, v_pre))
    o = jnp.concatenate(outs, axis=2)
    return o.transpose(0, 2, 1, 3).reshape(b, s, W_ATTN)


def setup_inputs(seed: int = 0) -> dict:
    key = jax.random.key(seed)
    ks = jax.random.split(key, 12)
    x = jax.random.normal(ks[0], (BATCH, SEQ, D_MODEL), jnp.float32)
    w_in = jax.random.normal(ks[1], (D_MODEL, D_IN), jnp.float32) * D_MODEL ** -0.5
    b_small = 0.02 * jax.random.normal(ks[2], (D_IN,), jnp.float32)
    b_forget = 2.0 + 2.0 * jax.random.uniform(ks[3], (N_HEADS,), jnp.float32)
    f_off = sum(SPLIT_SIZES[:6])
    b_in = b_small.at[f_off:f_off + N_HEADS].set(b_forget)
    w_dw = jax.random.normal(ks[4], (CONV_WIDTH, W_CONV), jnp.float32) * CONV_WIDTH ** -0.5
    b_dw = 0.02 * jax.random.normal(ks[5], (W_CONV,), jnp.float32)
    g_conv_norm = 1.0 + 0.05 * jax.random.normal(ks[6], (W_CONV,), jnp.float32)
    b_conv_norm = 0.02 * jax.random.normal(ks[7], (W_CONV,), jnp.float32)
    w_out = jax.random.normal(ks[8], (D_MIX, D_MODEL), jnp.float32) * (D_MIX ** -0.5) * DN_BETA
    b_out = 0.02 * jax.random.normal(ks[9], (D_MODEL,), jnp.float32)
    g_post = 1.0 + 0.05 * jax.random.normal(ks[10], (D_MODEL,), jnp.float32)
    b_post = 0.02 * jax.random.normal(ks[11], (D_MODEL,), jnp.float32)
    return {"x": x, "w_in": w_in, "b_in": b_in, "w_dw": w_dw, "b_dw": b_dw,
            "g_conv_norm": g_conv_norm, "b_conv_norm": b_conv_norm,
            "w_out": w_out, "b_out": b_out, "g_post": g_post, "b_post": b_post}


def reference(x, w_in, b_in, w_dw, b_dw, g_conv_norm, b_conv_norm, w_out, b_out, g_post, b_post):
    h = x
    for _ in range(DEPTH):
        z = jnp.einsum("bsd,de->bse", h, w_in) + b_in
        glu_val, glu_gate, gate_c, q, k, v, f_logit, gate_a = _split_columns(z)
        y_conv = _conformer_conv(glu_val, glu_gate, w_dw, b_dw, g_conv_norm, b_conv_norm)
        y_conv = y_conv * jax.nn.silu(gate_c)
        y_attn = _forgetting_attention(q, k, v, f_logit)
        y_attn = y_attn * jax.nn.silu(gate_a)
        y = jnp.concatenate([y_conv, y_attn], axis=-1)
        sub = jnp.einsum("bse,ed->bsd", y, w_out) + b_out
        h = _layer_norm(DN_ALPHA * h + sub, g_post, b_post)
    return h
```

```python
import functools
import math

import jax
import jax.numpy as jnp
from jax import lax
from jax.experimental import pallas as pl
from jax.experimental.pallas import tpu as pltpu

D_MODEL = 2048
SEQ = 8192
W_CONV = 2048
W_ATTN = 2048
HEAD_DIM = 128
N_HEADS = W_ATTN // HEAD_DIM
CONV_WIDTH = 31
LN_EPS = 1e-5
DN_ALPHA = 2.0 ** 0.25
LANES = 128
HALO = 32
MIB = 1024 * 1024

F32 = jnp.float32
BF16 = jnp.bfloat16


def _cparams(semantics, vmem_mib):
    return pltpu.CompilerParams(dimension_semantics=semantics,
                                vmem_limit_bytes=vmem_mib * MIB)


def _sigmoid(x):
    return 1.0 / (1.0 + jnp.exp(-x))


def _proj_kernel(x_ref, *refs, n_w, epilogue, transpose_out):
    w_refs, b_refs, o_ref = refs[:n_w], refs[n_w:2 * n_w], refs[2 * n_w]
    x = x_ref[...]
    zs = [jnp.dot(x, w[...], preferred_element_type=F32) + b[...]
          for w, b in zip(w_refs, b_refs)]
    y = epilogue(*zs)
    if transpose_out:
        y = y.T
    o_ref[...] = y.astype(o_ref.dtype)


def _proj(x_bf, ws, bs, epilogue, out_dtype, *, tm, tn, transpose_out=False, name):
    m, k = x_bf.shape
    n = ws[0].shape[1]
    n_w = len(ws)
    grid = (m // tm, n // tn)
    in_specs = [pl.BlockSpec((tm, k), lambda i, j: (i, 0))]
    in_specs += [pl.BlockSpec((k, tn), lambda i, j: (0, j)) for _ in ws]
    in_specs += [pl.BlockSpec((1, tn), lambda i, j: (0, j)) for _ in bs]
    if transpose_out:
        out_shape = jax.ShapeDtypeStruct((m // tm, n, tm), out_dtype)
        out_spec = pl.BlockSpec((None, tn, tm), lambda i, j: (i, j, 0))
    else:
        out_shape = jax.ShapeDtypeStruct((m, n), out_dtype)
        out_spec = pl.BlockSpec((tm, tn), lambda i, j: (i, j))
    return pl.pallas_call(
        functools.partial(_proj_kernel, n_w=n_w, epilogue=epilogue,
                          transpose_out=transpose_out),
        grid=grid, in_specs=in_specs, out_specs=out_spec, out_shape=out_shape,
        compiler_params=_cparams(("parallel", "arbitrary"), 48),
        name=name,
    )(x_bf, *ws, *bs)


def _cum_kernel(x_ref, wf_ref, bf_ref, cum_ref, ckb_ref, carry_ref, *, tb):
    @pl.when(pl.program_id(0) == 0)
    def _():
        carry_ref[...] = jnp.zeros_like(carry_ref)

    f = jnp.dot(x_ref[...], wf_ref[...], preferred_element_type=F32) + bf_ref[...]
    log_f = jnp.minimum(f, 0.0) - jnp.log1p(jnp.exp(-jnp.abs(f)))
    row = lax.broadcasted_iota(jnp.int32, (tb, tb), 0)
    col = lax.broadcasted_iota(jnp.int32, (tb, tb), 1)
    lower = (col <= row).astype(F32)
    cum = jnp.dot(lower, log_f, preferred_element_type=F32,
                  precision=lax.Precision.HIGHEST) + carry_ref[...]
    carry_ref[...] = cum[tb - 1:tb, :]
    cum_ref[...] = cum
    for h in range(N_HEADS):
        ckb_ref[h] = jnp.broadcast_to(cum[:, h:h + 1], (tb, LANES))


def _cum(x_bf, wf, bf, *, tb):
    m, k = x_bf.shape
    return pl.pallas_call(
        functools.partial(_cum_kernel, tb=tb),
        grid=(m // tb,),
        in_specs=[pl.BlockSpec((tb, k), lambda i: (i, 0)),
                  pl.BlockSpec((k, LANES), lambda i: (0, 0)),
                  pl.BlockSpec((1, LANES), lambda i: (0, 0))],
        out_specs=[pl.BlockSpec((tb, LANES), lambda i: (i, 0)),
                   pl.BlockSpec((N_HEADS, tb, LANES), lambda i: (0, i, 0))],
        out_shape=[jax.ShapeDtypeStruct((m, LANES), F32),
                   jax.ShapeDtypeStruct((N_HEADS, m, LANES), F32)],
        scratch_shapes=[pltpu.VMEM((1, LANES), F32)],
        compiler_params=_cparams(("arbitrary",), 32),
        name="forget_cumsum",
    )(x_bf, wf, bf)


def _conv_kernel(ucur_ref, uprev_ref, gc_ref, wdw_ref, bdw_ref, g_ref, b_ref, o_ref,
                 buf_ref, c_ref, *, tr, rc, lc):
    i = pl.program_id(0)
    halo = uprev_ref[...]
    buf_ref[0:HALO, :] = jnp.where(i == 0, 0.0, halo)
    buf_ref[HALO:HALO + tr, :] = ucur_ref[...]
    lead = HALO - (CONV_WIDTH - 1)

    for r0 in range(0, tr, rc):
        for c0 in range(0, W_CONV, lc):
            acc = jnp.broadcast_to(bdw_ref[:, c0:c0 + lc], (rc, lc))
            for b in range(8):
                n_a = (CONV_WIDTH - 1 - b) // 8 + 1
                win = buf_ref[r0 + lead + b:r0 + lead + b + rc + 8 * (n_a - 1), c0:c0 + lc]
                for a in range(n_a):
                    j = 8 * a + b
                    acc = acc + wdw_ref[j:j + 1, c0:c0 + lc] * win[8 * a:8 * a + rc, :]
            c_ref[r0:r0 + rc, c0:c0 + lc] = acc

    c = c_ref[...]
    mu = jnp.mean(c, axis=-1, keepdims=True)
    d = c - mu
    var = jnp.mean(d * d, axis=-1, keepdims=True)
    y = d * lax.rsqrt(var + LN_EPS) * g_ref[...] + b_ref[...]
    y = y * _sigmoid(y)
    o_ref[...] = (y * gc_ref[...]).astype(o_ref.dtype)


def _conv(u, gc, w_dw, b_dw, g_cn, b_cn, *, tr):
    s, w = u.shape
    vec = lambda: pl.BlockSpec((1, w), lambda i: (0, 0))
    return pl.pallas_call(
        functools.partial(_conv_kernel, tr=tr, rc=32, lc=256),
        grid=(s // tr,),
        in_specs=[pl.BlockSpec((tr, w), lambda i: (i, 0)),
                  pl.BlockSpec((HALO, w), lambda i: (jnp.maximum(i * (tr // HALO) - 1, 0), 0)),
                  pl.BlockSpec((tr, w), lambda i: (i, 0)),
                  pl.BlockSpec((CONV_WIDTH, w), lambda i: (0, 0)),
                  vec(), vec(), vec()],
        out_specs=pl.BlockSpec((tr, w), lambda i: (i, 0)),
        out_shape=jax.ShapeDtypeStruct((s, w), BF16),
        scratch_shapes=[pltpu.VMEM((HALO + tr, w), F32), pltpu.VMEM((tr, w), F32)],
        compiler_params=_cparams(("parallel",), 40),
        name="conformer_conv",
    )(u, u, gc, w_dw, b_dw, g_cn, b_cn)


def _attn_kernel(q_ref, k_ref, vt_ref, cq_ref, ckb_ref, ga_ref, o_ref,
                 acc_ref, m_ref, l_ref, *, tq, scale):
    i = pl.program_id(1)
    q = q_ref[...]
    cq = cq_ref[...]
    m_ref[...] = jnp.full_like(m_ref, -jnp.inf)
    l_ref[...] = jnp.zeros_like(l_ref)
    acc_ref[...] = jnp.zeros_like(acc_ref)

    def step(j, masked):
        k0 = pl.multiple_of(j * tq, tq)
        k_blk = k_ref[pl.ds(k0, tq), :]
        s = lax.dot_general(k_blk, q, (((1,), (1,)), ((), ())),
                            preferred_element_type=F32)
        ck = ckb_ref[pl.ds(k0, tq), :]
        ck = jnp.concatenate([ck] * (tq // LANES), axis=1)
        s = s * scale + (cq - ck)
        if masked:
            key_pos = lax.broadcasted_iota(jnp.int32, s.shape, 0)
            qry_pos = lax.broadcasted_iota(jnp.int32, s.shape, 1)
            s = jnp.where(key_pos <= qry_pos, s, -jnp.inf)
        m_prev = m_ref[...]
        m_new = jnp.maximum(m_prev, jnp.max(s, axis=0, keepdims=True))
        alpha = jnp.exp(m_prev - m_new)
        p = jnp.exp(s - m_new)
        l_ref[...] = alpha * l_ref[...] + jnp.sum(p, axis=0, keepdims=True)
        pv = jnp.dot(vt_ref[j], p.astype(BF16), preferred_element_type=F32)
        acc_ref[...] = alpha * acc_ref[...] + pv
        m_ref[...] = m_new

    def body(j, carry):
        step(j, False)
        return carry

    lax.fori_loop(0, i, body, 0)
    step(i, True)
    o = (acc_ref[...] / l_ref[...]).T
    o_ref[...] = (o * ga_ref[...]).astype(o_ref.dtype)


def _attn(qk, vt, cq, ckb, ga, *, tq):
    s = qk.shape[0]
    nq = s // tq
    return pl.pallas_call(
        functools.partial(_attn_kernel, tq=tq, scale=1.0 / math.sqrt(HEAD_DIM)),
        grid=(N_HEADS, nq),
        in_specs=[pl.BlockSpec((tq, HEAD_DIM), lambda h, i: (i, h)),
                  pl.BlockSpec((s, HEAD_DIM), lambda h, i: (0, N_HEADS + h)),
                  pl.BlockSpec((nq, HEAD_DIM, tq), lambda h, i: (0, h, 0)),
                  pl.BlockSpec((None, 1, tq), lambda h, i: (h, 0, i)),
                  pl.BlockSpec((None, s, LANES), lambda h, i: (h, 0, 0)),
                  pl.BlockSpec((tq, HEAD_DIM), lambda h, i: (i, h))],
        out_specs=pl.BlockSpec((tq, HEAD_DIM), lambda h, i: (i, h)),
        out_shape=jax.ShapeDtypeStruct((s, W_ATTN), BF16),
        scratch_shapes=[pltpu.VMEM((HEAD_DIM, tq), F32),
                        pltpu.VMEM((1, tq), F32),
                        pltpu.VMEM((1, tq), F32)],
        compiler_params=_cparams(("parallel", "arbitrary"), 48),
        name="forgetting_attention",
    )(qk, qk, vt, cq, ckb, ga)


def _out_kernel(yc_ref, ya_ref, w_ref, x_ref, bo_ref, g_ref, b_ref, o_ref):
    sub = jnp.dot(yc_ref[...], w_ref[0:W_CONV, :], preferred_element_type=F32)
    sub = sub + jnp.dot(ya_ref[...], w_ref[W_CONV:W_CONV + W_ATTN, :], preferred_element_type=F32)
    h = DN_ALPHA * x_ref[...] + (sub + bo_ref[...])
    mu = jnp.mean(h, axis=-1, keepdims=True)
    d = h - mu
    var = jnp.mean(d * d, axis=-1, keepdims=True)
    o_ref[...] = d * lax.rsqrt(var + LN_EPS) * g_ref[...] + b_ref[...]


def _out(yc, ya, w_out_bf, x2, b_out, g_post, b_post, *, tm):
    s, d = x2.shape
    e = w_out_bf.shape[0]
    vec = lambda: pl.BlockSpec((1, d), lambda i: (0, 0))
    return pl.pallas_call(
        _out_kernel,
        grid=(s // tm,),
        in_specs=[pl.BlockSpec((tm, W_CONV), lambda i: (i, 0)),
                  pl.BlockSpec((tm, W_ATTN), lambda i: (i, 0)),
                  pl.BlockSpec((e, d), lambda i: (0, 0), pipeline_mode=pl.Buffered(1)),
                  pl.BlockSpec((tm, d), lambda i: (i, 0)),
                  vec(), vec(), vec()],
        out_specs=pl.BlockSpec((tm, d), lambda i: (i, 0)),
        out_shape=jax.ShapeDtypeStruct((s, d), F32),
        compiler_params=_cparams(("parallel",), 56),
        name="out_proj_deepnorm",
    )(yc, ya, w_out_bf, x2, b_out, g_post, b_post)


def kernel(x, w_in, b_in, w_dw, b_dw, g_conv_norm, b_conv_norm, w_out, b_out, g_post, b_post):
    assert x.shape == (1, SEQ, D_MODEL)
    x2 = x[0]
    x_bf = x2.astype(BF16)
    row = lambda v: v.reshape(1, -1)

    def wcols(a, b):
        return w_in[:, a:b].astype(BF16), row(b_in[a:b])

    o_val, o_gate, o_gc, o_q, o_v, o_f = 0, 2048, 4096, 6144, 10240, 12288
    o_ga = o_f + N_HEADS
    w_val, b_val = wcols(o_val, o_gate)
    w_gate, b_gate = wcols(o_gate, o_gc)
    w_gc, b_gc = wcols(o_gc, o_q)
    w_qk, b_qk = wcols(o_q, o_v)
    w_v, b_v = wcols(o_v, o_f)
    w_ga, b_ga = wcols(o_ga, o_ga + W_ATTN)
    w_f = jnp.pad(w_in[:, o_f:o_ga], ((0, 0), (0, LANES - N_HEADS))).astype(BF16)
    b_f = row(jnp.pad(b_in[o_f:o_ga], (0, LANES - N_HEADS)))

    tq = 512
    silu = lambda z: z * _sigmoid(z)
    u = _proj(x_bf, [w_val, w_gate], [b_val, b_gate], lambda a, g: a * _sigmoid(g), F32,
              tm=1024, tn=512, name="proj_glu")
    gc = _proj(x_bf, [w_gc], [b_gc], silu, F32, tm=1024, tn=512, name="proj_gate_conv")
    qk = _proj(x_bf, [w_qk], [b_qk], lambda z: z, BF16, tm=1024, tn=512, name="proj_qk")
    vt = _proj(x_bf, [w_v], [b_v], lambda z: z, BF16, tm=tq, tn=512, transpose_out=True,
               name="proj_v_transposed")
    ga = _proj(x_bf, [w_ga], [b_ga], silu, F32, tm=1024, tn=512, name="proj_gate_attn")

    cum, ckb = _cum(x_bf, w_f, b_f, tb=512)
    cq = cum[:, :N_HEADS].T.reshape(N_HEADS, 1, SEQ)

    y_conv = _conv(u, gc, w_dw, row(b_dw), row(g_conv_norm), row(b_conv_norm), tr=256)
    y_attn = _attn(qk, vt, cq, ckb, ga, tq=tq)
    out = _out(y_conv, y_attn, w_out.astype(BF16), x2, row(b_out), row(g_post), row(b_post), tm=256)
    return out[None]
```

```python
import functools
import math

import jax
import jax.numpy as jnp
from jax import lax
from jax.experimental import pallas as pl
from jax.experimental.pallas import tpu as pltpu

D_MODEL = 2048
SEQ = 8192
W_CONV = 2048
W_ATTN = 2048
HEAD_DIM = 128
N_HEADS = W_ATTN // HEAD_DIM
CONV_WIDTH = 31
LN_EPS = 1e-5
DN_ALPHA = 2.0 ** 0.25
LOG2E = math.log2(math.e)
LANES = 128
SUBLANES = 8
HALO = 32
MIB = 1024 * 1024

F32 = jnp.float32
BF16 = jnp.bfloat16


def _cparams(semantics, vmem_mib):
    return pltpu.CompilerParams(dimension_semantics=semantics,
                                vmem_limit_bytes=vmem_mib * MIB)


def _sigmoid(x):
    return 1.0 / (1.0 + jnp.exp(-x))


def _proj_kernel(x_ref, *refs, n_w, epilogue, transpose_out):
    w_refs, b_refs, o_ref = refs[:n_w], refs[n_w:2 * n_w], refs[2 * n_w]
    x = x_ref[...]
    zs = [jnp.dot(x, w[...].astype(BF16), preferred_element_type=F32) + b[...]
          for w, b in zip(w_refs, b_refs)]
    y = epilogue(*zs)
    if transpose_out:
        y = y.T
    o_ref[...] = y.astype(o_ref.dtype)


def _proj(x_bf, w, col_offsets, bs, n, epilogue, out_dtype, *, tm, tn, transpose_out=False, name):
    m, k = x_bf.shape
    n_w = len(col_offsets)
    assert all(off % tn == 0 for off in col_offsets)
    grid = (m // tm, n // tn)
    in_specs = [pl.BlockSpec((tm, k), lambda i, j: (i, 0))]
    in_specs += [pl.BlockSpec((k, tn), functools.partial(lambda i, j, o: (0, o + j), o=off // tn))
                 for off in col_offsets]
    in_specs += [pl.BlockSpec((1, tn), lambda i, j: (0, j)) for _ in bs]
    if transpose_out:
        out_shape = jax.ShapeDtypeStruct((m // tm, n, tm), out_dtype)
        out_spec = pl.BlockSpec((None, tn, tm), lambda i, j: (i, j, 0))
    else:
        out_shape = jax.ShapeDtypeStruct((m, n), out_dtype)
        out_spec = pl.BlockSpec((tm, tn), lambda i, j: (i, j))
    return pl.pallas_call(
        functools.partial(_proj_kernel, n_w=n_w, epilogue=epilogue,
                          transpose_out=transpose_out),
        grid=grid, in_specs=in_specs, out_specs=out_spec, out_shape=out_shape,
        compiler_params=_cparams(("parallel", "arbitrary"), 56),
        name=name,
    )(x_bf, *([w] * n_w), *bs)


def _split3(c):
    hi = c.astype(BF16).astype(F32)
    r1 = c - hi
    mid = r1.astype(BF16).astype(F32)
    lo = (r1 - mid).astype(BF16).astype(F32)
    return hi, mid, lo


def _cum_kernel(x_ref, wf_ref, bf_ref, kaug_ref, qaug_ref, carry_ref, *, tb):
    @pl.when(pl.program_id(0) == 0)
    def _():
        carry_ref[...] = jnp.zeros_like(carry_ref)

    f = jnp.dot(x_ref[...], wf_ref[...], preferred_element_type=F32) + bf_ref[...]
    log_f = jnp.minimum(f, 0.0) - jnp.log1p(jnp.exp(-jnp.abs(f)))
    row = lax.broadcasted_iota(jnp.int32, (tb, tb), 0)
    col = lax.broadcasted_iota(jnp.int32, (tb, tb), 1)
    lower = (col <= row).astype(F32)
    cum = jnp.dot(lower, log_f, preferred_element_type=F32,
                  precision=lax.Precision.HIGHEST) + carry_ref[...]
    carry_ref[...] = cum[tb - 1:tb, :]

    cum2 = cum * LOG2E
    lane = lax.broadcasted_iota(jnp.int32, (tb, LANES), 1)
    for h in range(N_HEADS):
        hi, mid, lo = _split3(jnp.broadcast_to(cum2[:, h:h + 1], (tb, LANES)))
        k_cols = jnp.where(lane == 0, -hi, jnp.where(lane == 1, -mid, jnp.where(lane == 2, -lo,
                 jnp.where(lane < 6, 1.0, 0.0))))
        q_cols = jnp.where(lane < 3, 1.0, jnp.where(lane == 3, hi, jnp.where(lane == 4, mid,
                 jnp.where(lane == 5, lo, 0.0))))
        kaug_ref[h] = k_cols.astype(BF16)
        qaug_ref[h] = q_cols.astype(BF16)


def _cum(x_bf, wf, bf, *, tb):
    m, k = x_bf.shape
    aug_shape = jax.ShapeDtypeStruct((N_HEADS, m, LANES), BF16)
    aug_spec = pl.BlockSpec((N_HEADS, tb, LANES), lambda i: (0, i, 0))
    return pl.pallas_call(
        functools.partial(_cum_kernel, tb=tb),
        grid=(m // tb,),
        in_specs=[pl.BlockSpec((tb, k), lambda i: (i, 0)),
                  pl.BlockSpec((k, LANES), lambda i: (0, 0)),
                  pl.BlockSpec((1, LANES), lambda i: (0, 0))],
        out_specs=[aug_spec, aug_spec],
        out_shape=[aug_shape, aug_shape],
        scratch_shapes=[pltpu.VMEM((1, LANES), F32)],
        compiler_params=_cparams(("arbitrary",), 32),
        name="forget_cumsum",
    )(x_bf, wf, bf)


def _conv_kernel(ucur_ref, uprev_ref, gc_ref, wdw_ref, bdw_ref, g_ref, b_ref, o_ref,
                 buf_ref, c_ref, *, tr, rc, lc):
    i = pl.program_id(0)
    halo = uprev_ref[...]
    buf_ref[0:HALO, :] = jnp.where(i == 0, 0.0, halo)
    buf_ref[HALO:HALO + tr, :] = ucur_ref[...]
    lead = HALO - (CONV_WIDTH - 1)
    win_rows = rc + HALO

    for r0 in range(0, tr, rc):
        for c0 in range(0, W_CONV, lc):
            win = buf_ref[r0:r0 + win_rows, c0:c0 + lc]
            acc = jnp.broadcast_to(bdw_ref[:, c0:c0 + lc], (rc, lc))
            for b in range(SUBLANES):
                rot = win if b == 0 else pltpu.roll(win, win_rows - b, axis=0)
                for a in range(win_rows // SUBLANES):
                    j = SUBLANES * a + b - lead
                    if 0 <= j < CONV_WIDTH:
                        acc = acc + wdw_ref[j:j + 1, c0:c0 + lc] * rot[SUBLANES * a:SUBLANES * a + rc, :]
            c_ref[r0:r0 + rc, c0:c0 + lc] = acc

    c = c_ref[...]
    mu = jnp.mean(c, axis=-1, keepdims=True)
    d = c - mu
    var = jnp.mean(d * d, axis=-1, keepdims=True)
    y = d * lax.rsqrt(var + LN_EPS) * g_ref[...] + b_ref[...]
    y = y * _sigmoid(y)
    o_ref[...] = (y * gc_ref[...]).astype(o_ref.dtype)


def _conv(u, gc, w_dw, b_dw, g_cn, b_cn, *, tr):
    s, w = u.shape
    vec = lambda: pl.BlockSpec((1, w), lambda i: (0, 0))
    return pl.pallas_call(
        functools.partial(_conv_kernel, tr=tr, rc=64, lc=128),
        grid=(s // tr,),
        in_specs=[pl.BlockSpec((tr, w), lambda i: (i, 0)),
                  pl.BlockSpec((HALO, w), lambda i: (jnp.maximum(i * (tr // HALO) - 1, 0), 0)),
                  pl.BlockSpec((tr, w), lambda i: (i, 0)),
                  pl.BlockSpec((CONV_WIDTH, w), lambda i: (0, 0)),
                  vec(), vec(), vec()],
        out_specs=pl.BlockSpec((tr, w), lambda i: (i, 0)),
        out_shape=jax.ShapeDtypeStruct((s, w), BF16),
        scratch_shapes=[pltpu.VMEM((HALO + tr, w), F32), pltpu.VMEM((tr, w), F32)],
        compiler_params=_cparams(("parallel",), 40),
        name="conformer_conv",
    )(u, u, gc, w_dw, b_dw, g_cn, b_cn)


def _attn_kernel(q_ref, qaug_ref, k_ref, kaug_ref, vt_ref, ga_ref, o_ref,
                 s0_ref, s1_ref, acc_ref, m_ref, l_ref, *, tq, tkv):
    i = pl.program_id(1)
    qa = jnp.concatenate([q_ref[...], qaug_ref[...]], axis=1)
    m_ref[...] = jnp.full_like(m_ref, -jnp.inf)
    l_ref[...] = jnp.zeros_like(l_ref)
    acc_ref[...] = jnp.zeros_like(acc_ref)

    def scores(j, s_ref):
        r0 = pl.multiple_of(j * tkv, tkv)
        ka = jnp.concatenate([k_ref[pl.ds(r0, tkv), :], kaug_ref[pl.ds(r0, tkv), :]], axis=1)
        s_ref[...] = lax.dot_general(ka, qa, (((1,), (1,)), ((), ())),
                                     preferred_element_type=F32)

    def softmax_pv(j, s_ref, key_offset):
        s = s_ref[...]
        if key_offset is not None:
            key_pos = lax.broadcasted_iota(jnp.int32, s.shape, 0) + key_offset
            qry_pos = lax.broadcasted_iota(jnp.int32, s.shape, 1)
            s = jnp.where(key_pos <= qry_pos, s, -jnp.inf)
        m_prev = m_ref[...]
        m_new = jnp.maximum(m_prev, jnp.max(s, axis=0, keepdims=True))
        alpha = jnp.exp2(m_prev - m_new)
        p = jnp.exp2(s - m_new)
        l_ref[...] = alpha * l_ref[...] + jnp.sum(p, axis=0, keepdims=True)
        pv = jnp.dot(vt_ref[j], p.astype(BF16), preferred_element_type=F32)
        acc_ref[...] = alpha * acc_ref[...] + pv
        m_ref[...] = m_new

    scores(0, s0_ref)

    def body(p, carry):
        j = 2 * p
        scores(j + 1, s1_ref)
        softmax_pv(j, s0_ref, None)
        scores(j + 2, s0_ref)
        softmax_pv(j + 1, s1_ref, None)
        return carry

    lax.fori_loop(0, i, body, 0)
    scores(2 * i + 1, s1_ref)
    softmax_pv(2 * i, s0_ref, 0)
    softmax_pv(2 * i + 1, s1_ref, tkv)
    o = (acc_ref[...] / l_ref[...]).T
    o_ref[...] = (o * ga_ref[...]).astype(o_ref.dtype)


def _attn(q, k, vt, qaug, kaug, ga, *, tq, tkv):
    s = q.shape[0]
    assert tq == 2 * tkv and vt.shape == (s // tkv, W_ATTN, tkv)
    return pl.pallas_call(
        functools.partial(_attn_kernel, tq=tq, tkv=tkv),
        grid=(N_HEADS, s // tq),
        in_specs=[pl.BlockSpec((tq, HEAD_DIM), lambda h, i: (i, h)),
                  pl.BlockSpec((None, tq, LANES), lambda h, i: (h, i, 0)),
                  pl.BlockSpec((s, HEAD_DIM), lambda h, i: (0, h)),
                  pl.BlockSpec((None, s, LANES), lambda h, i: (h, 0, 0)),
                  pl.BlockSpec((s // tkv, HEAD_DIM, tkv), lambda h, i: (0, h, 0)),
                  pl.BlockSpec((tq, HEAD_DIM), lambda h, i: (i, h))],
        out_specs=pl.BlockSpec((tq, HEAD_DIM), lambda h, i: (i, h)),
        out_shape=jax.ShapeDtypeStruct((s, W_ATTN), BF16),
        scratch_shapes=[pltpu.VMEM((tkv, tq), F32),
                        pltpu.VMEM((tkv, tq), F32),
                        pltpu.VMEM((HEAD_DIM, tq), F32),
                        pltpu.VMEM((1, tq), F32),
                        pltpu.VMEM((1, tq), F32)],
        compiler_params=_cparams(("parallel", "arbitrary"), 48),
        name="forgetting_attention",
    )(q, qaug, k, kaug, vt, ga)


def _out_kernel(yc_ref, ya_ref, w_ref, x_ref, bo_ref, g_ref, b_ref, o_ref):
    sub = jnp.dot(yc_ref[...], w_ref[0:W_CONV, :], preferred_element_type=F32)
    sub = sub + jnp.dot(ya_ref[...], w_ref[W_CONV:W_CONV + W_ATTN, :], preferred_element_type=F32)
    h = DN_ALPHA * x_ref[...] + (sub + bo_ref[...])
    mu = jnp.mean(h, axis=-1, keepdims=True)
    d = h - mu
    var = jnp.mean(d * d, axis=-1, keepdims=True)
    o_ref[...] = d * lax.rsqrt(var + LN_EPS) * g_ref[...] + b_ref[...]


def _out(yc, ya, w_out_bf, x2, b_out, g_post, b_post, *, tm):
    s, d = x2.shape
    e = w_out_bf.shape[0]
    vec = lambda: pl.BlockSpec((1, d), lambda i: (0, 0))
    return pl.pallas_call(
        _out_kernel,
        grid=(s // tm,),
        in_specs=[pl.BlockSpec((tm, W_CONV), lambda i: (i, 0)),
                  pl.BlockSpec((tm, W_ATTN), lambda i: (i, 0)),
                  pl.BlockSpec((e, d), lambda i: (0, 0), pipeline_mode=pl.Buffered(1)),
                  pl.BlockSpec((tm, d), lambda i: (i, 0)),
                  vec(), vec(), vec()],
        out_specs=pl.BlockSpec((tm, d), lambda i: (i, 0)),
        out_shape=jax.ShapeDtypeStruct((s, d), F32),
        compiler_params=_cparams(("parallel",), 56),
        name="out_proj_deepnorm",
    )(yc, ya, w_out_bf, x2, b_out, g_post, b_post)


def kernel(x, w_in, b_in, w_dw, b_dw, g_conv_norm, b_conv_norm, w_out, b_out, g_post, b_post):
    assert x.shape == (1, SEQ, D_MODEL)
    x2 = x[0]
    x_bf = x2.astype(BF16)
    row = lambda v: v.reshape(1, -1)
    bias = lambda a, n: row(b_in[a:a + n])

    o_val, o_gate, o_gc, o_q, o_k, o_v, o_f = 0, 2048, 4096, 6144, 8192, 10240, 12288
    o_ga = o_f + N_HEADS
    w_ga = w_in[:, o_ga:o_ga + W_ATTN]
    w_f = jnp.pad(w_in[:, o_f:o_ga], ((0, 0), (0, LANES - N_HEADS))).astype(BF16)
    b_f = row(jnp.pad(b_in[o_f:o_ga], (0, LANES - N_HEADS)))

    tq, tkv = 1024, 512
    tm, tn = 2048, 512
    q_scale = LOG2E / math.sqrt(HEAD_DIM)
    silu = lambda z: z * _sigmoid(z)
    u = _proj(x_bf, w_in, [o_val, o_gate], [bias(o_val, W_CONV), bias(o_gate, W_CONV)], W_CONV,
              lambda a, g: a * _sigmoid(g), F32, tm=tm // 2, tn=tn, name="proj_glu")
    gc = _proj(x_bf, w_in, [o_gc], [bias(o_gc, W_CONV)], W_CONV, silu, F32,
               tm=tm, tn=tn, name="proj_gate_conv")
    q = _proj(x_bf, w_in, [o_q], [bias(o_q, W_ATTN)], W_ATTN, lambda z: z * q_scale, BF16,
              tm=tm, tn=tn, name="proj_q")
    k = _proj(x_bf, w_in, [o_k], [bias(o_k, W_ATTN)], W_ATTN, lambda z: z, BF16,
              tm=tm, tn=tn, name="proj_k")
    vt = _proj(x_bf, w_in, [o_v], [bias(o_v, W_ATTN)], W_ATTN, lambda z: z, BF16,
               tm=tkv, tn=tn, transpose_out=True, name="proj_v_transposed")
    ga = _proj(x_bf, w_ga, [0], [bias(o_ga, W_ATTN)], W_ATTN, silu, F32,
               tm=tm, tn=tn, name="proj_gate_attn")

    kaug, qaug = _cum(x_bf, w_f, b_f, tb=512)

    y_conv = _conv(u, gc, w_dw, row(b_dw), row(g_conv_norm), row(b_conv_norm), tr=256)
    y_attn = _attn(q, k, vt, qaug, kaug, ga, tq=tq, tkv=tkv)
    out = _out(y_conv, y_attn, w_out.astype(BF16), x2, row(b_out), row(g_post), row(b_post), tm=256)
    return out[None]
```

```python
import functools
import math

import jax
import jax.numpy as jnp
from jax import lax
from jax.experimental import pallas as pl
from jax.experimental.pallas import tpu as pltpu

D_MODEL = 2048
SEQ = 8192
W_CONV = 2048
W_ATTN = 2048
HEAD_DIM = 128
N_HEADS = W_ATTN // HEAD_DIM
CONV_WIDTH = 31
LN_EPS = 1e-5
DN_ALPHA = 2.0 ** 0.25
LOG2E = math.log2(math.e)
LANES = 128
SUBLANES = 8
HALO = 32
MIB = 1024 * 1024

F32 = jnp.float32
BF16 = jnp.bfloat16


def _cparams(semantics, vmem_mib):
    return pltpu.CompilerParams(dimension_semantics=semantics,
                                vmem_limit_bytes=vmem_mib * MIB)


def _sigmoid(x):
    return 1.0 / (1.0 + jnp.exp(-x))


_NT = (((1,), (1,)), ((), ()))


def _proj_kernel(x_ref, *refs, n_w, epilogue):
    w_refs, b_refs, o_ref = refs[:n_w], refs[n_w:2 * n_w], refs[2 * n_w]
    x = x_ref[...]
    zs = [lax.dot_general(x, w[...].astype(BF16), _NT, preferred_element_type=F32) + b[...]
          for w, b in zip(w_refs, b_refs)]
    o_ref[...] = epilogue(*zs).astype(o_ref.dtype)


def _proj(x_bf, wt, row_offsets, bs, n, epilogue, out_dtype, *, tm, tn, name):
    m, k = x_bf.shape
    n_w = len(row_offsets)
    assert all(off % tn == 0 for off in row_offsets)
    in_specs = [pl.BlockSpec((tm, k), lambda i, j: (i, 0))]
    in_specs += [pl.BlockSpec((tn, k), functools.partial(lambda i, j, o: (o + j, 0), o=off // tn))
                 for off in row_offsets]
    in_specs += [pl.BlockSpec((1, tn), lambda i, j: (0, j)) for _ in bs]
    return pl.pallas_call(
        functools.partial(_proj_kernel, n_w=n_w, epilogue=epilogue),
        grid=(m // tm, n // tn), in_specs=in_specs,
        out_specs=pl.BlockSpec((tm, tn), lambda i, j: (i, j)),
        out_shape=jax.ShapeDtypeStruct((m, n), out_dtype),
        compiler_params=_cparams(("parallel", "arbitrary"), 56),
        name=name,
    )(x_bf, *([wt] * n_w), *bs)


def _proj_t_kernel(w_ref, x_ref, b_ref, o_ref, wbf_ref):
    @pl.when(pl.program_id(1) == 0)
    def _():
        wbf_ref[...] = w_ref[...].astype(BF16)

    z = lax.dot_general(wbf_ref[...], x_ref[...], _NT, preferred_element_type=F32)
    o_ref[...] = (z + b_ref[...]).astype(o_ref.dtype)


def _proj_t(x_bf, wt, row_offset, b_col, n, *, ts, tn, name):
    m, k = x_bf.shape
    assert row_offset % tn == 0
    return pl.pallas_call(
        _proj_t_kernel,
        grid=(n // tn, m // ts),
        in_specs=[pl.BlockSpec((tn, k), lambda a, i: (row_offset // tn + a, 0)),
                  pl.BlockSpec((ts, k), lambda a, i: (i, 0)),
                  pl.BlockSpec((tn, 1), lambda a, i: (a, 0))],
        out_specs=pl.BlockSpec((None, tn, ts), lambda a, i: (i, a, 0)),
        out_shape=jax.ShapeDtypeStruct((m // ts, n, ts), BF16),
        scratch_shapes=[pltpu.VMEM((tn, k), BF16)],
        compiler_params=_cparams(("parallel", "arbitrary"), 48),
        name=name,
    )(wt, x_bf, b_col)


def _split3(c):
    hi = c.astype(BF16).astype(F32)
    r1 = c - hi
    mid = r1.astype(BF16).astype(F32)
    lo = (r1 - mid).astype(BF16).astype(F32)
    return hi, mid, lo


def _cum_kernel(x_ref, wf_ref, bf_ref, kaug_ref, qaug_ref, carry_ref, *, tb):
    @pl.when(pl.program_id(0) == 0)
    def _():
        carry_ref[...] = jnp.zeros_like(carry_ref)

    f = lax.dot_general(x_ref[...], wf_ref[...], _NT, preferred_element_type=F32) + bf_ref[...]
    log_f = jnp.minimum(f, 0.0) - jnp.log1p(jnp.exp(-jnp.abs(f)))
    row = lax.broadcasted_iota(jnp.int32, (tb, tb), 0)
    col = lax.broadcasted_iota(jnp.int32, (tb, tb), 1)
    lower = (col <= row).astype(F32)
    cum = jnp.dot(lower, log_f, preferred_element_type=F32,
                  precision=lax.Precision.HIGHEST) + carry_ref[...]
    carry_ref[...] = cum[tb - 1:tb, :]

    cum2 = cum * LOG2E
    lane = lax.broadcasted_iota(jnp.int32, (tb, LANES), 1)
    for h in range(N_HEADS):
        hi, mid, lo = _split3(jnp.broadcast_to(cum2[:, h:h + 1], (tb, LANES)))
        k_cols = jnp.where(lane == 0, -hi, jnp.where(lane == 1, -mid, jnp.where(lane == 2, -lo,
                 jnp.where(lane < 6, 1.0, 0.0))))
        q_cols = jnp.where(lane < 3, 1.0, jnp.where(lane == 3, hi, jnp.where(lane == 4, mid,
                 jnp.where(lane == 5, lo, 0.0))))
        kaug_ref[h] = k_cols.astype(BF16)
        qaug_ref[h] = q_cols.astype(BF16)


def _cum(x_bf, wf, bf, *, tb):
    m, k = x_bf.shape
    aug_shape = jax.ShapeDtypeStruct((N_HEADS, m, LANES), BF16)
    aug_spec = pl.BlockSpec((N_HEADS, tb, LANES), lambda i: (0, i, 0))
    return pl.pallas_call(
        functools.partial(_cum_kernel, tb=tb),
        grid=(m // tb,),
        in_specs=[pl.BlockSpec((tb, k), lambda i: (i, 0)),
                  pl.BlockSpec((LANES, k), lambda i: (0, 0)),
                  pl.BlockSpec((1, LANES), lambda i: (0, 0))],
        out_specs=[aug_spec, aug_spec],
        out_shape=[aug_shape, aug_shape],
        scratch_shapes=[pltpu.VMEM((1, LANES), F32)],
        compiler_params=_cparams(("arbitrary",), 32),
        name="forget_cumsum",
    )(x_bf, wf, bf)


def _conv_kernel(ucur_ref, uprev_ref, gc_ref, wdw_ref, bdw_ref, g_ref, b_ref, o_ref,
                 buf_ref, c_ref, *, tr, rc, lc):
    i = pl.program_id(0)
    halo = uprev_ref[...]
    buf_ref[0:HALO, :] = jnp.where(i == 0, 0.0, halo)
    buf_ref[HALO:HALO + tr, :] = ucur_ref[...]
    lead = HALO - (CONV_WIDTH - 1)
    win_rows = rc + HALO

    for r0 in range(0, tr, rc):
        for c0 in range(0, W_CONV, lc):
            win = buf_ref[r0:r0 + win_rows, c0:c0 + lc]
            acc = jnp.broadcast_to(bdw_ref[:, c0:c0 + lc], (rc, lc))
            for b in range(SUBLANES):
                rot = win if b == 0 else pltpu.roll(win, win_rows - b, axis=0)
                for a in range(win_rows // SUBLANES):
                    j = SUBLANES * a + b - lead
                    if 0 <= j < CONV_WIDTH:
                        acc = acc + wdw_ref[j:j + 1, c0:c0 + lc] * rot[SUBLANES * a:SUBLANES * a + rc, :]
            c_ref[r0:r0 + rc, c0:c0 + lc] = acc

    c = c_ref[...]
    mu = jnp.mean(c, axis=-1, keepdims=True)
    d = c - mu
    var = jnp.mean(d * d, axis=-1, keepdims=True)
    y = d * lax.rsqrt(var + LN_EPS) * g_ref[...] + b_ref[...]
    y = y * _sigmoid(y)
    o_ref[...] = (y * gc_ref[...]).astype(o_ref.dtype)


def _conv(u, gc, w_dw, b_dw, g_cn, b_cn, *, tr):
    s, w = u.shape
    vec = lambda: pl.BlockSpec((1, w), lambda i: (0, 0))
    return pl.pallas_call(
        functools.partial(_conv_kernel, tr=tr, rc=64, lc=128),
        grid=(s // tr,),
        in_specs=[pl.BlockSpec((tr, w), lambda i: (i, 0)),
                  pl.BlockSpec((HALO, w), lambda i: (jnp.maximum(i * (tr // HALO) - 1, 0), 0)),
                  pl.BlockSpec((tr, w), lambda i: (i, 0)),
                  pl.BlockSpec((CONV_WIDTH, w), lambda i: (0, 0)),
                  vec(), vec(), vec()],
        out_specs=pl.BlockSpec((tr, w), lambda i: (i, 0)),
        out_shape=jax.ShapeDtypeStruct((s, w), BF16),
        scratch_shapes=[pltpu.VMEM((HALO + tr, w), F32), pltpu.VMEM((tr, w), F32)],
        compiler_params=_cparams(("parallel",), 40),
        name="conformer_conv",
    )(u, u, gc, w_dw, b_dw, g_cn, b_cn)


def _attn_kernel(q_ref, qaug_ref, k_ref, kaug_ref, vt_ref, ga_ref, o_ref,
                 s0_ref, s1_ref, acc_ref, m_ref, l_ref, *, tq, tkv):
    i = pl.program_id(1)
    qa = jnp.concatenate([q_ref[...], qaug_ref[...]], axis=1)
    m_ref[...] = jnp.full_like(m_ref, -jnp.inf)
    l_ref[...] = jnp.zeros_like(l_ref)
    acc_ref[...] = jnp.zeros_like(acc_ref)

    def scores(j, s_ref, c0=0):
        r0 = pl.multiple_of(j * tkv, tkv)
        ka = jnp.concatenate([k_ref[pl.ds(r0, tkv), :], kaug_ref[pl.ds(r0, tkv), :]], axis=1)
        s_ref[:, c0:] = lax.dot_general(ka, qa[c0:], _NT, preferred_element_type=F32)

    def softmax_pv(j, s_ref, c0=0, c1=tq, key_offset=None):
        s = s_ref[:, c0:c1]
        if key_offset is not None:
            key_pos = lax.broadcasted_iota(jnp.int32, s.shape, 0) + key_offset
            qry_pos = lax.broadcasted_iota(jnp.int32, s.shape, 1) + c0
            s = jnp.where(key_pos <= qry_pos, s, -jnp.inf)
        m_prev = m_ref[:, c0:c1]
        m_new = jnp.maximum(m_prev, jnp.max(s, axis=0, keepdims=True))
        alpha = jnp.exp2(m_prev - m_new)
        p = jnp.exp2(s - m_new)
        l_ref[:, c0:c1] = alpha * l_ref[:, c0:c1] + jnp.sum(p, axis=0, keepdims=True)
        pv = jnp.dot(vt_ref[j], p.astype(BF16), preferred_element_type=F32)
        acc_ref[:, c0:c1] = alpha * acc_ref[:, c0:c1] + pv
        m_ref[:, c0:c1] = m_new

    scores(0, s0_ref)

    def body(p, carry):
        j = 2 * p
        scores(j + 1, s1_ref)
        softmax_pv(j, s0_ref)
        scores(j + 2, s0_ref)
        softmax_pv(j + 1, s1_ref)
        return carry

    lax.fori_loop(0, i, body, 0)
    scores(2 * i + 1, s1_ref, c0=tkv)
    softmax_pv(2 * i, s0_ref, 0, tkv, key_offset=0)
    softmax_pv(2 * i, s0_ref, tkv, tq)
    softmax_pv(2 * i + 1, s1_ref, tkv, tq, key_offset=tkv)
    o = (acc_ref[...] / l_ref[...]).T
    o_ref[...] = (o * ga_ref[...]).astype(o_ref.dtype)


def _attn(q, k, vt, qaug, kaug, ga, *, tq, tkv):
    s = q.shape[0]
    assert tq == 2 * tkv and vt.shape == (s // tkv, W_ATTN, tkv)
    return pl.pallas_call(
        functools.partial(_attn_kernel, tq=tq, tkv=tkv),
        grid=(N_HEADS, s // tq),
        in_specs=[pl.BlockSpec((tq, HEAD_DIM), lambda h, i: (i, h)),
                  pl.BlockSpec((None, tq, LANES), lambda h, i: (h, i, 0)),
                  pl.BlockSpec((s, HEAD_DIM), lambda h, i: (0, h)),
                  pl.BlockSpec((None, s, LANES), lambda h, i: (h, 0, 0)),
                  pl.BlockSpec((s // tkv, HEAD_DIM, tkv), lambda h, i: (0, h, 0)),
                  pl.BlockSpec((tq, HEAD_DIM), lambda h, i: (i, h))],
        out_specs=pl.BlockSpec((tq, HEAD_DIM), lambda h, i: (i, h)),
        out_shape=jax.ShapeDtypeStruct((s, W_ATTN), BF16),
        scratch_shapes=[pltpu.VMEM((tkv, tq), F32),
                        pltpu.VMEM((tkv, tq), F32),
                        pltpu.VMEM((HEAD_DIM, tq), F32),
                        pltpu.VMEM((1, tq), F32),
                        pltpu.VMEM((1, tq), F32)],
        compiler_params=_cparams(("parallel", "arbitrary"), 48),
        name="forgetting_attention",
    )(q, qaug, k, kaug, vt, ga)


def _out_kernel(yc_ref, ya_ref, w_ref, x_ref, bo_ref, g_ref, b_ref, o_ref):
    sub = jnp.dot(yc_ref[...], w_ref[0:W_CONV, :], preferred_element_type=F32)
    sub = sub + jnp.dot(ya_ref[...], w_ref[W_CONV:W_CONV + W_ATTN, :], preferred_element_type=F32)
    h = DN_ALPHA * x_ref[...] + (sub + bo_ref[...])
    mu = jnp.mean(h, axis=-1, keepdims=True)
    d = h - mu
    var = jnp.mean(d * d, axis=-1, keepdims=True)
    o_ref[...] = d * lax.rsqrt(var + LN_EPS) * g_ref[...] + b_ref[...]


def _out(yc, ya, w_out_bf, x2, b_out, g_post, b_post, *, tm):
    s, d = x2.shape
    e = w_out_bf.shape[0]
    vec = lambda: pl.BlockSpec((1, d), lambda i: (0, 0))
    return pl.pallas_call(
        _out_kernel,
        grid=(s // tm,),
        in_specs=[pl.BlockSpec((tm, W_CONV), lambda i: (i, 0)),
                  pl.BlockSpec((tm, W_ATTN), lambda i: (i, 0)),
                  pl.BlockSpec((e, d), lambda i: (0, 0), pipeline_mode=pl.Buffered(1)),
                  pl.BlockSpec((tm, d), lambda i: (i, 0)),
                  vec(), vec(), vec()],
        out_specs=pl.BlockSpec((tm, d), lambda i: (i, 0)),
        out_shape=jax.ShapeDtypeStruct((s, d), F32),
        compiler_params=_cparams(("parallel",), 56),
        name="out_proj_deepnorm",
    )(yc, ya, w_out_bf, x2, b_out, g_post, b_post)


def kernel(x, w_in, b_in, w_dw, b_dw, g_conv_norm, b_conv_norm, w_out, b_out, g_post, b_post):
    assert x.shape == (1, SEQ, D_MODEL)
    x2 = x[0]
    x_bf = x2.astype(BF16)
    row = lambda v: v.reshape(1, -1)
    bias = lambda a, n: row(b_in[a:a + n])

    o_val, o_gate, o_gc, o_q, o_k, o_v, o_f = 0, 2048, 4096, 6144, 8192, 10240, 12288
    o_ga = o_f + N_HEADS
    wt = w_in.T
    wt_ga = wt[o_ga:o_ga + W_ATTN]
    wt_f = jnp.pad(wt[o_f:o_ga], ((0, LANES - N_HEADS), (0, 0))).astype(BF16)
    b_f = row(jnp.pad(b_in[o_f:o_ga], (0, LANES - N_HEADS)))

    tq, tkv = 1024, 512
    tm, tn = 2048, 512
    q_scale = LOG2E / math.sqrt(HEAD_DIM)
    silu = lambda z: z * _sigmoid(z)
    u = _proj(x_bf, wt, [o_val, o_gate], [bias(o_val, W_CONV), bias(o_gate, W_CONV)], W_CONV,
              lambda a, g: a * _sigmoid(g), F32, tm=tm // 2, tn=tn, name="proj_glu")
    gc = _proj(x_bf, wt, [o_gc], [bias(o_gc, W_CONV)], W_CONV, silu, F32,
               tm=tm, tn=tn, name="proj_gate_conv")
    q = _proj(x_bf, wt, [o_q], [bias(o_q, W_ATTN)], W_ATTN, lambda z: z * q_scale, BF16,
              tm=tm, tn=tn, name="proj_q")
    k = _proj(x_bf, wt, [o_k], [bias(o_k, W_ATTN)], W_ATTN, lambda z: z, BF16,
              tm=tm, tn=tn, name="proj_k")
    vt = _proj_t(x_bf, wt, o_v, b_in[o_v:o_v + W_ATTN].reshape(-1, 1), W_ATTN,
                 ts=tkv, tn=1024, name="proj_v_transposed")
    ga = _proj(x_bf, wt_ga, [0], [bias(o_ga, W_ATTN)], W_ATTN, silu, F32,
               tm=tm, tn=tn, name="proj_gate_attn")

    kaug, qaug = _cum(x_bf, wt_f, b_f, tb=512)

    y_conv = _conv(u, gc, w_dw, row(b_dw), row(g_conv_norm), row(b_conv_norm), tr=256)
    y_attn = _attn(q, k, vt, qaug, kaug, ga, tq=tq, tkv=tkv)
    out = _out(y_conv, y_attn, w_out.astype(BF16), x2, row(b_out), row(g_post), row(b_post), tm=256)
    return out[None]
```

```python
import functools
import math

import jax
import jax.numpy as jnp
from jax import lax
from jax.experimental import pallas as pl
from jax.experimental.pallas import tpu as pltpu

D_MODEL = 2048
SEQ = 8192
W_CONV = 2048
W_ATTN = 2048
HEAD_DIM = 128
N_HEADS = W_ATTN // HEAD_DIM
CONV_WIDTH = 31
LN_EPS = 1e-5
DN_ALPHA = 2.0 ** 0.25
LOG2E = math.log2(math.e)
LANES = 128
SUBLANES = 8
HALO = 32
MIB = 1024 * 1024

F32_EXP2_UNDERFLOW = 152.0
SKIP_SLACK = 2.0
NORM_INFLATE = 1.02

F32 = jnp.float32
BF16 = jnp.bfloat16

_NT = (((1,), (1,)), ((), ()))


def _cparams(semantics, vmem_mib):
    return pltpu.CompilerParams(dimension_semantics=semantics,
                                vmem_limit_bytes=vmem_mib * MIB)


def _sigmoid(x):
    return 1.0 / (1.0 + jnp.exp(-x))


def _layer_norm(h, g, b):
    mu = jnp.mean(h, axis=-1, keepdims=True)
    d = h - mu
    var = jnp.mean(d * d, axis=-1, keepdims=True)
    return d * lax.rsqrt(var + LN_EPS) * g + b


def _proj_kernel(x_ref, *refs, n_w, epilogue, head_norms):
    w_refs, b_refs, o_ref = refs[:n_w], refs[n_w:2 * n_w], refs[2 * n_w]
    x = x_ref[...]
    zs = [lax.dot_general(x, w[...].astype(BF16), _NT, preferred_element_type=F32) + b[...]
          for w, b in zip(w_refs, b_refs)]
    y = epilogue(*zs).astype(o_ref.dtype)
    o_ref[...] = y
    if head_norms:
        nrm_ref = refs[2 * n_w + 1]
        sq = y.astype(F32)
        sq = sq * sq
        lane = lax.broadcasted_iota(jnp.int32, (1, LANES), 1)
        out = jnp.zeros((1, LANES), F32)
        for hh in range(y.shape[1] // HEAD_DIM):
            n2 = jnp.sum(sq[:, hh * HEAD_DIM:(hh + 1) * HEAD_DIM], axis=-1, keepdims=True)
            out = jnp.where(lane == hh, jnp.max(n2, axis=0, keepdims=True), out)
        nrm_ref[0] = out


def _proj(x_bf, wt, row_offsets, bs, n, epilogue, out_dtype, *, tm, tn, name, head_norms=False):
    m, k = x_bf.shape
    n_w = len(row_offsets)
    assert all(off % tn == 0 for off in row_offsets)
    in_specs = [pl.BlockSpec((tm, k), lambda i, j: (i, 0))]
    in_specs += [pl.BlockSpec((tn, k), functools.partial(lambda i, j, o: (o + j, 0), o=off // tn))
                 for off in row_offsets]
    in_specs += [pl.BlockSpec((1, tn), lambda i, j: (0, j)) for _ in bs]
    out_specs = [pl.BlockSpec((tm, tn), lambda i, j: (i, j))]
    out_shape = [jax.ShapeDtypeStruct((m, n), out_dtype)]
    nj = n // tn
    if head_norms:
        out_specs.append(pl.BlockSpec((1, 1, LANES), lambda i, j: (i * nj + j, 0, 0)))
        out_shape.append(jax.ShapeDtypeStruct((m // tm * nj, 1, LANES), F32))
    res = pl.pallas_call(
        functools.partial(_proj_kernel, n_w=n_w, epilogue=epilogue, head_norms=head_norms),
        grid=(m // tm, nj), in_specs=in_specs, out_specs=out_specs, out_shape=out_shape,
        compiler_params=_cparams(("parallel", "arbitrary"), 56),
        name=name,
    )(x_bf, *([wt] * n_w), *bs)
    if not head_norms:
        return res[0]
    heads_per_tile = tn // HEAD_DIM
    norms = res[1].reshape(m // tm, nj, LANES)[:, :, :heads_per_tile].reshape(m // tm, n // HEAD_DIM)
    return res[0], norms


def _proj_t_kernel(w_ref, x_ref, b_ref, o_ref, wbf_ref):
    @pl.when(pl.program_id(1) == 0)
    def _():
        wbf_ref[...] = w_ref[...].astype(BF16)

    z = lax.dot_general(wbf_ref[...], x_ref[...], _NT, preferred_element_type=F32)
    o_ref[...] = (z + b_ref[...]).astype(o_ref.dtype)


def _proj_t(x_bf, wt, row_offset, b_col, n, *, ts, tn, name):
    m, k = x_bf.shape
    assert row_offset % tn == 0
    return pl.pallas_call(
        _proj_t_kernel,
        grid=(n // tn, m // ts),
        in_specs=[pl.BlockSpec((tn, k), lambda a, i: (row_offset // tn + a, 0)),
                  pl.BlockSpec((ts, k), lambda a, i: (i, 0)),
                  pl.BlockSpec((tn, 1), lambda a, i: (a, 0))],
        out_specs=pl.BlockSpec((None, tn, ts), lambda a, i: (i, a, 0)),
        out_shape=jax.ShapeDtypeStruct((m // ts, n, ts), BF16),
        scratch_shapes=[pltpu.VMEM((tn, k), BF16)],
        compiler_params=_cparams(("parallel", "arbitrary"), 48),
        name=name,
    )(wt, x_bf, b_col)


_FIELD = N_HEADS


def _split3(c):
    hi = c.astype(BF16).astype(F32)
    r1 = c - hi
    mid = r1.astype(BF16).astype(F32)
    lo = (r1 - mid).astype(BF16).astype(F32)
    return hi, mid, lo


def _cum_kernel(x_ref, wf_ref, bf_ref, kaug_ref, qall_ref, cfirst_ref, clast_ref, carry_ref, *, tb):
    @pl.when(pl.program_id(0) == 0)
    def _():
        carry_ref[...] = jnp.zeros_like(carry_ref)

    f = lax.dot_general(x_ref[...], wf_ref[...], _NT, preferred_element_type=F32) + bf_ref[...]
    log_f = jnp.minimum(f, 0.0) - jnp.log1p(jnp.exp(-jnp.abs(f)))
    row = lax.broadcasted_iota(jnp.int32, (tb, tb), 0)
    col = lax.broadcasted_iota(jnp.int32, (tb, tb), 1)
    lower = (col <= row).astype(F32)
    cum = jnp.dot(lower, log_f, preferred_element_type=F32,
                  precision=lax.Precision.HIGHEST) + carry_ref[...]
    carry_ref[...] = cum[tb - 1:tb, :]

    cum2 = cum * LOG2E
    cfirst_ref[0] = cum2[0:1, :]
    clast_ref[0] = cum2[tb - 1:tb, :]
    hi, mid, lo = _split3(cum2)
    lane = lax.broadcasted_iota(jnp.int32, (tb, LANES), 1)
    field = lambda v, n: pltpu.roll(v, n * _FIELD, axis=1)
    k_cols = jnp.where(lane < _FIELD, -hi,
             jnp.where(lane < 2 * _FIELD, -field(mid, 1),
             jnp.where(lane < 3 * _FIELD, -field(lo, 2),
             jnp.where(lane < 6 * _FIELD, 1.0, 0.0))))
    q_cols = jnp.where(lane < 3 * _FIELD, 1.0,
             jnp.where(lane < 4 * _FIELD, field(hi, 3),
             jnp.where(lane < 5 * _FIELD, field(mid, 4),
             jnp.where(lane < 6 * _FIELD, field(lo, 5), 0.0))))
    kaug_ref[...] = k_cols.astype(BF16)
    qall_ref[...] = q_cols.astype(BF16)


def _cum(x_bf, wf, bf, *, tb):
    m, k = x_bf.shape
    aug_shape = jax.ShapeDtypeStruct((m, LANES), BF16)
    aug_spec = pl.BlockSpec((tb, LANES), lambda i: (i, 0))
    edge_shape = jax.ShapeDtypeStruct((m // tb, 1, LANES), F32)
    edge_spec = pl.BlockSpec((1, 1, LANES), lambda i: (i, 0, 0))
    return pl.pallas_call(
        functools.partial(_cum_kernel, tb=tb),
        grid=(m // tb,),
        in_specs=[pl.BlockSpec((tb, k), lambda i: (i, 0)),
                  pl.BlockSpec((LANES, k), lambda i: (0, 0)),
                  pl.BlockSpec((1, LANES), lambda i: (0, 0))],
        out_specs=[aug_spec, aug_spec, edge_spec, edge_spec],
        out_shape=[aug_shape, aug_shape, edge_shape, edge_shape],
        scratch_shapes=[pltpu.VMEM((1, LANES), F32)],
        compiler_params=_cparams(("arbitrary",), 32),
        name="forget_cumsum",
    )(x_bf, wf, bf)


def _plan_kernel(qn2_ref, kn2_ref, cfirst_ref, clast_ref, o_ref, *, nq, q_per_norm_block, kv_per_q):
    kn = jnp.sqrt(jnp.max(kn2_ref[...], axis=0, keepdims=True))
    clast = clast_ref[...]
    n_kv = clast.shape[0]
    blk = lax.broadcasted_iota(jnp.int32, (n_kv, LANES), 0)
    assert kv_per_q == 2
    pair_shift = jnp.full((1, LANES), 1, jnp.int32)
    for i in range(nq):
        qn = jnp.sqrt(qn2_ref[i // q_per_norm_block:i // q_per_norm_block + 1, :])
        bound = (2.0 * NORM_INFLATE) * qn * kn + (cfirst_ref[kv_per_q * i:kv_per_q * i + 1, :] - clast)
        skip = (bound < -(F32_EXP2_UNDERFLOW + SKIP_SLACK)) & (blk < kv_per_q * i)
        first = jnp.min(jnp.where(skip, n_kv, blk), axis=0, keepdims=True)
        o_ref[i:i + 1, :] = lax.shift_right_logical(first, pair_shift)


def _plan(qn2, kn2, cfirst, clast, *, nq, q_per_norm_block, kv_per_q):
    full = lambda a: pl.BlockSpec(a.shape, lambda: (0,) * a.ndim)
    args = (qn2, kn2, cfirst, clast)
    return pl.pallas_call(
        functools.partial(_plan_kernel, nq=nq, q_per_norm_block=q_per_norm_block, kv_per_q=kv_per_q),
        in_specs=[full(a) for a in args],
        out_specs=pl.BlockSpec((nq, LANES), lambda: (0, 0)),
        out_shape=jax.ShapeDtypeStruct((nq, LANES), jnp.int32),
        name="attention_plan",
    )(*args)


def _attn_kernel(first_ref, q_ref, qall_ref, k_ref, kaug_ref, vt_ref, ga_ref, o_ref,
                 s0_ref, s1_ref, acc_ref, m_ref, l_ref, *, tq, tkv):
    h = pl.program_id(0)
    i = pl.program_id(1)
    lane = lax.broadcasted_iota(jnp.int32, (tq, LANES), 1)
    qall = qall_ref[...]
    own = (lane & (_FIELD - 1)) == h
    qa = jnp.concatenate([q_ref[...], jnp.where(own, qall, jnp.zeros_like(qall))], axis=1)
    m_ref[...] = jnp.full_like(m_ref, -jnp.inf)
    l_ref[...] = jnp.zeros_like(l_ref)
    acc_ref[...] = jnp.zeros_like(acc_ref)

    def scores(j, s_ref, c0=0):
        r0 = pl.multiple_of(j * tkv, tkv)
        ka = jnp.concatenate([k_ref[pl.ds(r0, tkv), :], kaug_ref[pl.ds(r0, tkv), :]], axis=1)
        s_ref[:, c0:] = lax.dot_general(ka, qa[c0:], _NT, preferred_element_type=F32)

    def softmax_pv(j, s_ref, c0=0, c1=tq, key_offset=None):
        s = s_ref[:, c0:c1]
        if key_offset is not None:
            key_pos = lax.broadcasted_iota(jnp.int32, s.shape, 0) + key_offset
            qry_pos = lax.broadcasted_iota(jnp.int32, s.shape, 1) + c0
            s = jnp.where(key_pos <= qry_pos, s, -jnp.inf)
        m_prev = m_ref[:, c0:c1]
        m_new = jnp.maximum(m_prev, jnp.max(s, axis=0, keepdims=True))
        alpha = jnp.exp2(m_prev - m_new)
        p = jnp.exp2(s - m_new)
        l_ref[:, c0:c1] = alpha * l_ref[:, c0:c1] + jnp.sum(p, axis=0, keepdims=True)
        pv = jnp.dot(vt_ref[j], p.astype(BF16), preferred_element_type=F32)
        acc_ref[:, c0:c1] = alpha * acc_ref[:, c0:c1] + pv
        m_ref[:, c0:c1] = m_new

    p0 = first_ref[h * pl.num_programs(1) + i]
    scores(2 * p0, s0_ref)

    def body(p, carry):
        j = 2 * p
        scores(j + 1, s1_ref)
        softmax_pv(j, s0_ref)
        scores(j + 2, s0_ref)
        softmax_pv(j + 1, s1_ref)
        return carry

    lax.fori_loop(p0, i, body, 0)
    scores(2 * i + 1, s1_ref, c0=tkv)
    softmax_pv(2 * i, s0_ref, 0, tkv, key_offset=0)
    softmax_pv(2 * i, s0_ref, tkv, tq)
    softmax_pv(2 * i + 1, s1_ref, tkv, tq, key_offset=tkv)
    o = (acc_ref[...] / l_ref[...]).T
    o_ref[...] = (o * ga_ref[...]).astype(o_ref.dtype)


def _attn(first_pair, q, k, vt, qall, kaug, ga, *, tq, tkv):
    s = q.shape[0]
    assert tq == 2 * tkv and vt.shape == (s // tkv, W_ATTN, tkv)
    grid_spec = pltpu.PrefetchScalarGridSpec(
        num_scalar_prefetch=1,
        grid=(N_HEADS, s // tq),
        in_specs=[pl.BlockSpec((tq, HEAD_DIM), lambda h, i, f: (i, h)),
                  pl.BlockSpec((tq, LANES), lambda h, i, f: (i, 0)),
                  pl.BlockSpec((s, HEAD_DIM), lambda h, i, f: (0, h)),
                  pl.BlockSpec((s, LANES), lambda h, i, f: (0, 0)),
                  pl.BlockSpec((s // tkv, HEAD_DIM, tkv), lambda h, i, f: (0, h, 0)),
                  pl.BlockSpec((tq, HEAD_DIM), lambda h, i, f: (i, h))],
        out_specs=pl.BlockSpec((tq, HEAD_DIM), lambda h, i, f: (i, h)),
        scratch_shapes=[pltpu.VMEM((tkv, tq), F32),
                        pltpu.VMEM((tkv, tq), F32),
                        pltpu.VMEM((HEAD_DIM, tq), F32),
                        pltpu.VMEM((1, tq), F32),
                        pltpu.VMEM((1, tq), F32)])
    return pl.pallas_call(
        functools.partial(_attn_kernel, tq=tq, tkv=tkv),
        grid_spec=grid_spec,
        out_shape=jax.ShapeDtypeStruct((s, W_ATTN), BF16),
        compiler_params=_cparams(("parallel", "arbitrary"), 48),
        name="forgetting_attention",
    )(first_pair, q, qall, k, kaug, vt, ga)


def _conv_kernel(ucur_ref, uprev_ref, gc_ref, wdw_ref, bdw_ref, g_ref, b_ref, o_ref,
                 buf_ref, c_ref, *, tr, rc, lc):
    i = pl.program_id(0)
    buf_ref[0:HALO, :] = jnp.where(i == 0, 0.0, uprev_ref[...])
    buf_ref[HALO:HALO + tr, :] = ucur_ref[...]
    lead = HALO - (CONV_WIDTH - 1)
    win_rows = rc + HALO

    for r0 in range(0, tr, rc):
        for c0 in range(0, W_CONV, lc):
            win = buf_ref[r0:r0 + win_rows, c0:c0 + lc]
            acc = jnp.broadcast_to(bdw_ref[:, c0:c0 + lc], (rc, lc))
            for b in range(SUBLANES):
                rot = win if b == 0 else pltpu.roll(win, win_rows - b, axis=0)
                for a in range(win_rows // SUBLANES):
                    j = SUBLANES * a + b - lead
                    if 0 <= j < CONV_WIDTH:
                        acc = acc + wdw_ref[j:j + 1, c0:c0 + lc] * rot[SUBLANES * a:SUBLANES * a + rc, :]
            c_ref[r0:r0 + rc, c0:c0 + lc] = acc

    y = _layer_norm(c_ref[...], g_ref[...], b_ref[...])
    y = y * _sigmoid(y)
    o_ref[...] = (y * gc_ref[...]).astype(o_ref.dtype)


def _conv(u, gc, w_dw, b_dw, g_cn, b_cn, *, tr):
    s, w = u.shape
    vec = lambda: pl.BlockSpec((1, w), lambda i: (0, 0))
    return pl.pallas_call(
        functools.partial(_conv_kernel, tr=tr, rc=64, lc=128),
        grid=(s // tr,),
        in_specs=[pl.BlockSpec((tr, w), lambda i: (i, 0)),
                  pl.BlockSpec((HALO, w), lambda i: (jnp.maximum(i * (tr // HALO) - 1, 0), 0)),
                  pl.BlockSpec((tr, w), lambda i: (i, 0)),
                  pl.BlockSpec((CONV_WIDTH, w), lambda i: (0, 0)),
                  vec(), vec(), vec()],
        out_specs=pl.BlockSpec((tr, w), lambda i: (i, 0)),
        out_shape=jax.ShapeDtypeStruct((s, w), BF16),
        scratch_shapes=[pltpu.VMEM((HALO + tr, w), F32), pltpu.VMEM((tr, w), F32)],
        compiler_params=_cparams(("parallel",), 40),
        name="conformer_conv",
    )(u, u, gc, w_dw, b_dw, g_cn, b_cn)


def _out_kernel(yc_ref, ya_ref, w_ref, x_ref, bo_ref, g_ref, b_ref, o_ref):
    sub = jnp.dot(yc_ref[...], w_ref[0:W_CONV, :], preferred_element_type=F32)
    sub = sub + jnp.dot(ya_ref[...], w_ref[W_CONV:W_CONV + W_ATTN, :], preferred_element_type=F32)
    h = DN_ALPHA * x_ref[...] + (sub + bo_ref[...])
    o_ref[...] = _layer_norm(h, g_ref[...], b_ref[...])


def _out(yc, ya, w_out_bf, x2, b_out, g_post, b_post, *, tm):
    s, d = x2.shape
    e = w_out_bf.shape[0]
    vec = lambda: pl.BlockSpec((1, d), lambda i: (0, 0))
    return pl.pallas_call(
        _out_kernel,
        grid=(s // tm,),
        in_specs=[pl.BlockSpec((tm, W_CONV), lambda i: (i, 0)),
                  pl.BlockSpec((tm, W_ATTN), lambda i: (i, 0)),
                  pl.BlockSpec((e, d), lambda i: (0, 0), pipeline_mode=pl.Buffered(1)),
                  pl.BlockSpec((tm, d), lambda i: (i, 0)),
                  vec(), vec(), vec()],
        out_specs=pl.BlockSpec((tm, d), lambda i: (i, 0)),
        out_shape=jax.ShapeDtypeStruct((s, d), F32),
        compiler_params=_cparams(("parallel",), 56),
        name="out_proj_deepnorm",
    )(yc, ya, w_out_bf, x2, b_out, g_post, b_post)


def kernel(x, w_in, b_in, w_dw, b_dw, g_conv_norm, b_conv_norm, w_out, b_out, g_post, b_post):
    assert x.shape == (1, SEQ, D_MODEL)
    x2 = x[0]
    x_bf = x2.astype(BF16)
    row = lambda v: v.reshape(1, -1)
    bias = lambda a, n: row(b_in[a:a + n])
    lane_pad = lambda a: jnp.pad(a, ((0, 0), (0, LANES - a.shape[1])))

    o_val, o_gate, o_gc, o_q, o_k, o_v, o_f = 0, 2048, 4096, 6144, 8192, 10240, 12288
    o_ga = o_f + N_HEADS
    wt = w_in.T
    wt_ga = wt[o_ga:o_ga + W_ATTN]
    wt_f = jnp.pad(wt[o_f:o_ga], ((0, LANES - N_HEADS), (0, 0))).astype(BF16)
    b_f = lane_pad(bias(o_f, N_HEADS))

    tq, tkv = 1024, 512
    tm, tn = 2048, 512
    q_scale = LOG2E / math.sqrt(HEAD_DIM)
    silu = lambda z: z * _sigmoid(z)
    u = _proj(x_bf, wt, [o_val, o_gate], [bias(o_val, W_CONV), bias(o_gate, W_CONV)], W_CONV,
              lambda a, g: a * _sigmoid(g), F32, tm=tm // 2, tn=tn, name="proj_glu")
    gc = _proj(x_bf, wt, [o_gc], [bias(o_gc, W_CONV)], W_CONV, silu, F32,
               tm=tm, tn=tn, name="proj_gate_conv")
    q, qn2 = _proj(x_bf, wt, [o_q], [bias(o_q, W_ATTN)], W_ATTN, lambda z: z * q_scale, BF16,
                   tm=tm, tn=tn, name="proj_q", head_norms=True)
    k, kn2 = _proj(x_bf, wt, [o_k], [bias(o_k, W_ATTN)], W_ATTN, lambda z: z, BF16,
                   tm=tm, tn=tn, name="proj_k", head_norms=True)
    vt = _proj_t(x_bf, wt, o_v, b_in[o_v:o_v + W_ATTN].reshape(-1, 1), W_ATTN,
                 ts=tkv, tn=1024, name="proj_v_transposed")
    ga = _proj(x_bf, wt_ga, [0], [bias(o_ga, W_ATTN)], W_ATTN, silu, F32,
               tm=tm, tn=tn, name="proj_gate_attn")

    kaug, qall, cfirst, clast = _cum(x_bf, wt_f, b_f, tb=tkv)
    first_pair = _plan(lane_pad(qn2), lane_pad(kn2), cfirst[:, 0], clast[:, 0],
                       nq=SEQ // tq, q_per_norm_block=tm // tq, kv_per_q=tq // tkv)
    first_pair = first_pair[:, :N_HEADS].T.reshape(-1)

    y_conv = _conv(u, gc, w_dw, row(b_dw), row(g_conv_norm), row(b_conv_norm), tr=256)
    y_attn = _attn(first_pair, q, k, vt, qall, kaug, ga, tq=tq, tkv=tkv)
    out = _out(y_conv, y_attn, w_out.astype(BF16), x2, row(b_out), row(g_post), row(b_post), tm=256)
    return out[None]
```

```python
import functools
import math

import jax
import jax.numpy as jnp
from jax import lax
from jax.experimental import pallas as pl
from jax.experimental.pallas import tpu as pltpu

D_MODEL = 2048
SEQ = 8192
W_CONV = 2048
W_ATTN = 2048
HEAD_DIM = 128
N_HEADS = W_ATTN // HEAD_DIM
CONV_WIDTH = 31
LN_EPS = 1e-5
DN_ALPHA = 2.0 ** 0.25
LOG2E = math.log2(math.e)
LANES = 128
SUBLANES = 8
HALO = 32
MIB = 1024 * 1024

F32_EXP2_UNDERFLOW = 152.0
SKIP_SLACK = 2.0
NORM_INFLATE = 1.02

F32 = jnp.float32
BF16 = jnp.bfloat16

_NT = (((1,), (1,)), ((), ()))


def _cparams(semantics, vmem_mib):
    return pltpu.CompilerParams(dimension_semantics=semantics,
                                vmem_limit_bytes=vmem_mib * MIB)


def _sigmoid(x):
    return 1.0 / (1.0 + jnp.exp(-x))


def _layer_norm(h, g, b):
    mu = jnp.mean(h, axis=-1, keepdims=True)
    d = h - mu
    var = jnp.mean(d * d, axis=-1, keepdims=True)
    return d * lax.rsqrt(var + LN_EPS) * g + b


def _proj_kernel(x_ref, *refs, n_w, epilogue, head_norms):
    w_refs, b_refs, o_ref = refs[:n_w], refs[n_w:2 * n_w], refs[2 * n_w]
    x = x_ref[...]
    zs = [lax.dot_general(x, w[...].astype(BF16), _NT, preferred_element_type=F32) + b[...]
          for w, b in zip(w_refs, b_refs)]
    y = epilogue(*zs).astype(o_ref.dtype)
    o_ref[...] = y
    if head_norms:
        nrm_ref = refs[2 * n_w + 1]
        sq = y.astype(F32)
        sq = sq * sq
        lane = lax.broadcasted_iota(jnp.int32, (1, LANES), 1)
        out = jnp.zeros((1, LANES), F32)
        for hh in range(y.shape[1] // HEAD_DIM):
            n2 = jnp.sum(sq[:, hh * HEAD_DIM:(hh + 1) * HEAD_DIM], axis=-1, keepdims=True)
            out = jnp.where(lane == hh, jnp.max(n2, axis=0, keepdims=True), out)
        nrm_ref[0] = out


def _proj(x_bf, wt, row_offsets, bs, n, epilogue, out_dtype, *, tm, tn, name, head_norms=False):
    m, k = x_bf.shape
    n_w = len(row_offsets)
    assert all(off % tn == 0 for off in row_offsets)
    in_specs = [pl.BlockSpec((tm, k), lambda i, j: (i, 0))]
    in_specs += [pl.BlockSpec((tn, k), functools.partial(lambda i, j, o: (o + j, 0), o=off // tn))
                 for off in row_offsets]
    in_specs += [pl.BlockSpec((1, tn), lambda i, j: (0, j)) for _ in bs]
    out_specs = [pl.BlockSpec((tm, tn), lambda i, j: (i, j))]
    out_shape = [jax.ShapeDtypeStruct((m, n), out_dtype)]
    nj = n // tn
    if head_norms:
        out_specs.append(pl.BlockSpec((1, 1, LANES), lambda i, j: (i * nj + j, 0, 0)))
        out_shape.append(jax.ShapeDtypeStruct((m // tm * nj, 1, LANES), F32))
    res = pl.pallas_call(
        functools.partial(_proj_kernel, n_w=n_w, epilogue=epilogue, head_norms=head_norms),
        grid=(m // tm, nj), in_specs=in_specs, out_specs=out_specs, out_shape=out_shape,
        compiler_params=_cparams(("parallel", "arbitrary"), 56),
        name=name,
    )(x_bf, *([wt] * n_w), *bs)
    if not head_norms:
        return res[0]
    heads_per_tile = tn // HEAD_DIM
    norms = res[1].reshape(m // tm, nj, LANES)[:, :, :heads_per_tile].reshape(m // tm, n // HEAD_DIM)
    return res[0], norms


def _proj_t_kernel(w_ref, x_ref, b_ref, o_ref, wbf_ref):
    @pl.when(pl.program_id(1) == 0)
    def _():
        wbf_ref[...] = w_ref[...].astype(BF16)

    z = lax.dot_general(wbf_ref[...], x_ref[...], _NT, preferred_element_type=F32)
    o_ref[...] = (z + b_ref[...]).astype(o_ref.dtype)


def _proj_t(x_bf, wt, row_offset, b_col, n, *, ts, tn, name):
    m, k = x_bf.shape
    assert row_offset % tn == 0
    return pl.pallas_call(
        _proj_t_kernel,
        grid=(n // tn, m // ts),
        in_specs=[pl.BlockSpec((tn, k), lambda a, i: (row_offset // tn + a, 0)),
                  pl.BlockSpec((ts, k), lambda a, i: (i, 0)),
                  pl.BlockSpec((tn, 1), lambda a, i: (a, 0))],
        out_specs=pl.BlockSpec((None, tn, ts), lambda a, i: (i, a, 0)),
        out_shape=jax.ShapeDtypeStruct((m // ts, n, ts), BF16),
        scratch_shapes=[pltpu.VMEM((tn, k), BF16)],
        compiler_params=_cparams(("parallel", "arbitrary"), 48),
        name=name,
    )(wt, x_bf, b_col)


_FIELD = N_HEADS


def _split3(c):
    hi = c.astype(BF16).astype(F32)
    r1 = c - hi
    mid = r1.astype(BF16).astype(F32)
    lo = (r1 - mid).astype(BF16).astype(F32)
    return hi, mid, lo


def _cum_kernel(x_ref, wf_ref, bf_ref, kaug_ref, qall_ref, cfirst_ref, clast_ref, carry_ref, *, tb):
    @pl.when(pl.program_id(0) == 0)
    def _():
        carry_ref[...] = jnp.zeros_like(carry_ref)

    f = lax.dot_general(x_ref[...], wf_ref[...], _NT, preferred_element_type=F32) + bf_ref[...]
    log_f = jnp.minimum(f, 0.0) - jnp.log1p(jnp.exp(-jnp.abs(f)))
    row = lax.broadcasted_iota(jnp.int32, (tb, tb), 0)
    col = lax.broadcasted_iota(jnp.int32, (tb, tb), 1)
    lower = (col <= row).astype(F32)
    cum = jnp.dot(lower, log_f, preferred_element_type=F32,
                  precision=lax.Precision.HIGHEST) + carry_ref[...]
    carry_ref[...] = cum[tb - 1:tb, :]

    cum2 = cum * LOG2E
    cfirst_ref[0] = cum2[0:1, :]
    clast_ref[0] = cum2[tb - 1:tb, :]
    hi, mid, lo = _split3(cum2)
    lane = lax.broadcasted_iota(jnp.int32, (tb, LANES), 1)
    field = lambda v, n: pltpu.roll(v, n * _FIELD, axis=1)
    k_cols = jnp.where(lane < _FIELD, -hi,
             jnp.where(lane < 2 * _FIELD, -field(mid, 1),
             jnp.where(lane < 3 * _FIELD, -field(lo, 2),
             jnp.where(lane < 6 * _FIELD, 1.0, 0.0))))
    q_cols = jnp.where(lane < 3 * _FIELD, 1.0,
             jnp.where(lane < 4 * _FIELD, field(hi, 3),
             jnp.where(lane < 5 * _FIELD, field(mid, 4),
             jnp.where(lane < 6 * _FIELD, field(lo, 5), 0.0))))
    kaug_ref[...] = k_cols.astype(BF16)
    qall_ref[...] = q_cols.astype(BF16)


def _cum(x_bf, wf, bf, *, tb):
    m, k = x_bf.shape
    aug_shape = jax.ShapeDtypeStruct((m, LANES), BF16)
    aug_spec = pl.BlockSpec((tb, LANES), lambda i: (i, 0))
    edge_shape = jax.ShapeDtypeStruct((m // tb, 1, LANES), F32)
    edge_spec = pl.BlockSpec((1, 1, LANES), lambda i: (i, 0, 0))
    return pl.pallas_call(
        functools.partial(_cum_kernel, tb=tb),
        grid=(m // tb,),
        in_specs=[pl.BlockSpec((tb, k), lambda i: (i, 0)),
                  pl.BlockSpec((LANES, k), lambda i: (0, 0)),
                  pl.BlockSpec((1, LANES), lambda i: (0, 0))],
        out_specs=[aug_spec, aug_spec, edge_spec, edge_spec],
        out_shape=[aug_shape, aug_shape, edge_shape, edge_shape],
        scratch_shapes=[pltpu.VMEM((1, LANES), F32)],
        compiler_params=_cparams(("arbitrary",), 32),
        name="forget_cumsum",
    )(x_bf, wf, bf)


def _plan_kernel(qn2_ref, kn2_ref, cfirst_ref, clast_ref, o_ref, *, nq, q_per_norm_block, kv_per_q):
    kn = jnp.sqrt(jnp.max(kn2_ref[...], axis=0, keepdims=True))
    clast = clast_ref[...]
    n_kv = clast.shape[0]
    blk = lax.broadcasted_iota(jnp.int32, (n_kv, LANES), 0)
    assert kv_per_q == 2
    pair_shift = jnp.full((1, LANES), 1, jnp.int32)
    for i in range(nq):
        qn = jnp.sqrt(qn2_ref[i // q_per_norm_block:i // q_per_norm_block + 1, :])
        bound = (2.0 * NORM_INFLATE) * qn * kn + (cfirst_ref[kv_per_q * i:kv_per_q * i + 1, :] - clast)
        skip = (bound < -(F32_EXP2_UNDERFLOW + SKIP_SLACK)) & (blk < kv_per_q * i)
        first = jnp.min(jnp.where(skip, n_kv, blk), axis=0, keepdims=True)
        o_ref[i:i + 1, :] = lax.shift_right_logical(first, pair_shift)


def _plan(qn2, kn2, cfirst, clast, *, nq, q_per_norm_block, kv_per_q):
    full = lambda a: pl.BlockSpec(a.shape, lambda: (0,) * a.ndim)
    args = (qn2, kn2, cfirst, clast)
    return pl.pallas_call(
        functools.partial(_plan_kernel, nq=nq, q_per_norm_block=q_per_norm_block, kv_per_q=kv_per_q),
        in_specs=[full(a) for a in args],
        out_specs=pl.BlockSpec((nq, LANES), lambda: (0, 0)),
        out_shape=jax.ShapeDtypeStruct((nq, LANES), jnp.int32),
        name="attention_plan",
    )(*args)


def _attn_kernel(first_ref, q_ref, qall_ref, k_ref, kaug_ref, vt_ref, ga_ref, o_ref,
                 qa0_ref, qa1_ref, s0_ref, s1_ref, acc_ref, m_ref, l_ref, *, tq, tkv, nq):
    h = pl.program_id(0)
    lane = lax.broadcasted_iota(jnp.int32, (tq, LANES), 1)
    own = (lane & (_FIELD - 1)) == h

    def load_queries(i, qa_ref):
        r0 = pl.multiple_of(i * tq, tq)
        qall = qall_ref[pl.ds(r0, tq), :]
        qa_ref[...] = jnp.concatenate(
            [q_ref[pl.ds(r0, tq), :], jnp.where(own, qall, jnp.zeros_like(qall))], axis=1)

    def scores(j, qa_ref, s_ref, c0=0):
        r0 = pl.multiple_of(j * tkv, tkv)
        ka = jnp.concatenate([k_ref[pl.ds(r0, tkv), :], kaug_ref[pl.ds(r0, tkv), :]], axis=1)
        s_ref[:, c0:] = lax.dot_general(ka, qa_ref[c0:, :], _NT, preferred_element_type=F32)

    def softmax_pv(j, s_ref, c0=0, c1=tq, key_offset=None):
        s = s_ref[:, c0:c1]
        if key_offset is not None:
            key_pos = lax.broadcasted_iota(jnp.int32, s.shape, 0) + key_offset
            qry_pos = lax.broadcasted_iota(jnp.int32, s.shape, 1) + c0
            s = jnp.where(key_pos <= qry_pos, s, -jnp.inf)
        m_prev = m_ref[:, c0:c1]
        m_new = jnp.maximum(m_prev, jnp.max(s, axis=0, keepdims=True))
        alpha = jnp.exp2(m_prev - m_new)
        p = jnp.exp2(s - m_new)
        l_ref[:, c0:c1] = alpha * l_ref[:, c0:c1] + jnp.sum(p, axis=0, keepdims=True)
        pv = jnp.dot(vt_ref[j], p.astype(BF16), preferred_element_type=F32)
        acc_ref[:, c0:c1] = alpha * acc_ref[:, c0:c1] + pv
        m_ref[:, c0:c1] = m_new

    def first_pair(i):
        return first_ref[h * nq + i]

    def query_block(i, qa_ref, qa_next_ref):
        def pair(p, carry):
            j = 2 * p
            scores(j + 1, qa_ref, s1_ref)
            softmax_pv(j, s0_ref)
            scores(j + 2, qa_ref, s0_ref)
            softmax_pv(j + 1, s1_ref)
            return carry

        m_ref[...] = jnp.full_like(m_ref, -jnp.inf)
        l_ref[...] = jnp.zeros_like(l_ref)
        acc_ref[...] = jnp.zeros_like(acc_ref)
        lax.fori_loop(first_pair(i), i, pair, 0)
        scores(2 * i + 1, qa_ref, s1_ref, c0=tkv)
        softmax_pv(2 * i, s0_ref, 0, tkv, key_offset=0)
        softmax_pv(2 * i, s0_ref, tkv, tq)
        nxt = jnp.minimum(i + 1, nq - 1)
        load_queries(nxt, qa_next_ref)
        scores(2 * first_pair(nxt), qa_next_ref, s0_ref)
        softmax_pv(2 * i + 1, s1_ref, tkv, tq, key_offset=tkv)
        rows = pl.ds(pl.multiple_of(i * tq, tq), tq)
        o = (acc_ref[...] / l_ref[...]).T
        o_ref[rows, :] = (o * ga_ref[rows, :]).astype(o_ref.dtype)

    def two_query_blocks(t, carry):
        query_block(2 * t, qa0_ref, qa1_ref)
        query_block(2 * t + 1, qa1_ref, qa0_ref)
        return carry

    assert nq % 2 == 0
    load_queries(0, qa0_ref)
    scores(2 * first_pair(0), qa0_ref, s0_ref)
    lax.fori_loop(0, nq // 2, two_query_blocks, 0)


def _attn(first_pair, q, k, vt, qall, kaug, ga, *, tq, tkv):
    s = q.shape[0]
    assert tq == 2 * tkv and vt.shape == (s // tkv, W_ATTN, tkv)
    head = lambda h, f: (0, h)
    grid_spec = pltpu.PrefetchScalarGridSpec(
        num_scalar_prefetch=1,
        grid=(N_HEADS,),
        in_specs=[pl.BlockSpec((s, HEAD_DIM), head),
                  pl.BlockSpec((s, LANES), lambda h, f: (0, 0)),
                  pl.BlockSpec((s, HEAD_DIM), head),
                  pl.BlockSpec((s, LANES), lambda h, f: (0, 0)),
                  pl.BlockSpec((s // tkv, HEAD_DIM, tkv), lambda h, f: (0, h, 0)),
                  pl.BlockSpec((s, HEAD_DIM), head)],
        out_specs=pl.BlockSpec((s, HEAD_DIM), head),
        scratch_shapes=[pltpu.VMEM((tq, 2 * HEAD_DIM), BF16),
                        pltpu.VMEM((tq, 2 * HEAD_DIM), BF16),
                        pltpu.VMEM((tkv, tq), F32),
                        pltpu.VMEM((tkv, tq), F32),
                        pltpu.VMEM((HEAD_DIM, tq), F32),
                        pltpu.VMEM((1, tq), F32),
                        pltpu.VMEM((1, tq), F32)])
    return pl.pallas_call(
        functools.partial(_attn_kernel, tq=tq, tkv=tkv, nq=s // tq),
        grid_spec=grid_spec,
        out_shape=jax.ShapeDtypeStruct((s, W_ATTN), BF16),
        compiler_params=_cparams(("parallel",), 56),
        name="forgetting_attention",
    )(first_pair, q, qall, k, kaug, vt, ga)


def _conv_kernel(ucur_ref, uprev_ref, gc_ref, wdw_ref, bdw_ref, g_ref, b_ref, o_ref,
                 buf_ref, c_ref, *, tr, rc, lc):
    i = pl.program_id(0)
    buf_ref[0:HALO, :] = jnp.where(i == 0, 0.0, uprev_ref[...])
    buf_ref[HALO:HALO + tr, :] = ucur_ref[...]
    lead = HALO - (CONV_WIDTH - 1)
    win_rows = rc + HALO

    for r0 in range(0, tr, rc):
        for c0 in range(0, W_CONV, lc):
            win = buf_ref[r0:r0 + win_rows, c0:c0 + lc]
            acc = jnp.broadcast_to(bdw_ref[:, c0:c0 + lc], (rc, lc))
            for b in range(SUBLANES):
                rot = win if b == 0 else pltpu.roll(win, win_rows - b, axis=0)
                for a in range(win_rows // SUBLANES):
                    j = SUBLANES * a + b - lead
                    if 0 <= j < CONV_WIDTH:
                        acc = acc + wdw_ref[j:j + 1, c0:c0 + lc] * rot[SUBLANES * a:SUBLANES * a + rc, :]
            c_ref[r0:r0 + rc, c0:c0 + lc] = acc

    y = _layer_norm(c_ref[...], g_ref[...], b_ref[...])
    y = y * _sigmoid(y)
    o_ref[...] = (y * gc_ref[...]).astype(o_ref.dtype)


def _conv(u, gc, w_dw, b_dw, g_cn, b_cn, *, tr):
    s, w = u.shape
    vec = lambda: pl.BlockSpec((1, w), lambda i: (0, 0))
    return pl.pallas_call(
        functools.partial(_conv_kernel, tr=tr, rc=64, lc=128),
        grid=(s // tr,),
        in_specs=[pl.BlockSpec((tr, w), lambda i: (i, 0)),
                  pl.BlockSpec((HALO, w), lambda i: (jnp.maximum(i * (tr // HALO) - 1, 0), 0)),
                  pl.BlockSpec((tr, w), lambda i: (i, 0)),
                  pl.BlockSpec((CONV_WIDTH, w), lambda i: (0, 0)),
                  vec(), vec(), vec()],
        out_specs=pl.BlockSpec((tr, w), lambda i: (i, 0)),
        out_shape=jax.ShapeDtypeStruct((s, w), BF16),
        scratch_shapes=[pltpu.VMEM((HALO + tr, w), F32), pltpu.VMEM((tr, w), F32)],
        compiler_params=_cparams(("parallel",), 40),
        name="conformer_conv",
    )(u, u, gc, w_dw, b_dw, g_cn, b_cn)


def _out_kernel(yc_ref, ya_ref, w_ref, x_ref, bo_ref, g_ref, b_ref, o_ref):
    sub = jnp.dot(yc_ref[...], w_ref[0:W_CONV, :], preferred_element_type=F32)
    sub = sub + jnp.dot(ya_ref[...], w_ref[W_CONV:W_CONV + W_ATTN, :], preferred_element_type=F32)
    h = DN_ALPHA * x_ref[...] + (sub + bo_ref[...])
    o_ref[...] = _layer_norm(h, g_ref[...], b_ref[...])


def _out(yc, ya, w_out_bf, x2, b_out, g_post, b_post, *, tm):
    s, d = x2.shape
    e = w_out_bf.shape[0]
    vec = lambda: pl.BlockSpec((1, d), lambda i: (0, 0))
    return pl.pallas_call(
        _out_kernel,
        grid=(s // tm,),
        in_specs=[pl.BlockSpec((tm, W_CONV), lambda i: (i, 0)),
                  pl.BlockSpec((tm, W_ATTN), lambda i: (i, 0)),
                  pl.BlockSpec((e, d), lambda i: (0, 0), pipeline_mode=pl.Buffered(1)),
                  pl.BlockSpec((tm, d), lambda i: (i, 0)),
                  vec(), vec(), vec()],
        out_specs=pl.BlockSpec((tm, d), lambda i: (i, 0)),
        out_shape=jax.ShapeDtypeStruct((s, d), F32),
        compiler_params=_cparams(("parallel",), 56),
        name="out_proj_deepnorm",
    )(yc, ya, w_out_bf, x2, b_out, g_post, b_post)


def kernel(x, w_in, b_in, w_dw, b_dw, g_conv_norm, b_conv_norm, w_out, b_out, g_post, b_post):
    assert x.shape == (1, SEQ, D_MODEL)
    x2 = x[0]
    x_bf = x2.astype(BF16)
    row = lambda v: v.reshape(1, -1)
    bias = lambda a, n: row(b_in[a:a + n])
    lane_pad = lambda a: jnp.pad(a, ((0, 0), (0, LANES - a.shape[1])))

    o_val, o_gate, o_gc, o_q, o_k, o_v, o_f = 0, 2048, 4096, 6144, 8192, 10240, 12288
    o_ga = o_f + N_HEADS
    wt = w_in.T
    wt_ga = wt[o_ga:o_ga + W_ATTN]
    wt_f = jnp.pad(wt[o_f:o_ga], ((0, LANES - N_HEADS), (0, 0))).astype(BF16)
    b_f = lane_pad(bias(o_f, N_HEADS))

    tq, tkv = 1024, 512
    tm, tn = 2048, 512
    q_scale = LOG2E / math.sqrt(HEAD_DIM)
    silu = lambda z: z * _sigmoid(z)
    u = _proj(x_bf, wt, [o_val, o_gate], [bias(o_val, W_CONV), bias(o_gate, W_CONV)], W_CONV,
              lambda a, g: a * _sigmoid(g), F32, tm=tm // 2, tn=tn, name="proj_glu")
    gc = _proj(x_bf, wt, [o_gc], [bias(o_gc, W_CONV)], W_CONV, silu, F32,
               tm=tm, tn=tn, name="proj_gate_conv")
    q, qn2 = _proj(x_bf, wt, [o_q], [bias(o_q, W_ATTN)], W_ATTN, lambda z: z * q_scale, BF16,
                   tm=tm, tn=tn, name="proj_q", head_norms=True)
    k, kn2 = _proj(x_bf, wt, [o_k], [bias(o_k, W_ATTN)], W_ATTN, lambda z: z, BF16,
                   tm=tm, tn=tn, name="proj_k", head_norms=True)
    vt = _proj_t(x_bf, wt, o_v, b_in[o_v:o_v + W_ATTN].reshape(-1, 1), W_ATTN,
                 ts=tkv, tn=1024, name="proj_v_transposed")
    ga = _proj(x_bf, wt_ga, [0], [bias(o_ga, W_ATTN)], W_ATTN, silu, F32,
               tm=tm, tn=tn, name="proj_gate_attn")

    kaug, qall, cfirst, clast = _cum(x_bf, wt_f, b_f, tb=tkv)
    first_pair = _plan(lane_pad(qn2), lane_pad(kn2), cfirst[:, 0], clast[:, 0],
                       nq=SEQ // tq, q_per_norm_block=tm // tq, kv_per_q=tq // tkv)
    first_pair = first_pair[:, :N_HEADS].T.reshape(-1)

    y_conv = _conv(u, gc, w_dw, row(b_dw), row(g_conv_norm), row(b_conv_norm), tr=256)
    y_attn = _attn(first_pair, q, k, vt, qall, kaug, ga, tq=tq, tkv=tkv)
    out = _out(y_conv, y_attn, w_out.astype(BF16), x2, row(b_out), row(g_post), row(b_post), tm=256)
    return out[None]
```

```python
import functools
import math

import jax
import jax.numpy as jnp
from jax import lax
from jax.experimental import pallas as pl
from jax.experimental.pallas import tpu as pltpu

D_MODEL = 2048
SEQ = 8192
W_CONV = 2048
W_ATTN = 2048
HEAD_DIM = 128
N_HEADS = W_ATTN // HEAD_DIM
CONV_WIDTH = 31
LN_EPS = 1e-5
DN_ALPHA = 2.0 ** 0.25
LOG2E = math.log2(math.e)
LANES = 128
SUBLANES = 8
HALO = 32
MIB = 1024 * 1024

F32_EXP2_UNDERFLOW = 152.0
SKIP_SLACK = 2.0
NORM_INFLATE = 1.02

F32 = jnp.float32
BF16 = jnp.bfloat16

_NT = (((1,), (1,)), ((), ()))


def _cparams(semantics, vmem_mib):
    return pltpu.CompilerParams(dimension_semantics=semantics,
                                vmem_limit_bytes=vmem_mib * MIB)


def _sigmoid(x):
    return 1.0 / (1.0 + jnp.exp(-x))


def _layer_norm(h, g, b):
    mu = jnp.mean(h, axis=-1, keepdims=True)
    d = h - mu
    var = jnp.mean(d * d, axis=-1, keepdims=True)
    return d * lax.rsqrt(var + LN_EPS) * g + b


def _proj_kernel(x_ref, *refs, n_w, epilogue, head_norms, n_split):
    w_refs, b_refs, o_ref = refs[:n_w], refs[n_w:2 * n_w], refs[2 * n_w]
    ws = [w[...].astype(BF16) for w in w_refs]
    rows_per = o_ref.shape[0] // n_split
    for r in range(n_split):
        rows = slice(r * rows_per, (r + 1) * rows_per)
        zs = [lax.dot_general(x_ref[rows, :], w, _NT, preferred_element_type=F32) + b[...]
              for w, b in zip(ws, b_refs)]
        o_ref[rows, :] = epilogue(*zs).astype(o_ref.dtype)
    if head_norms:
        y = o_ref[...]
        nrm_ref = refs[2 * n_w + 1]
        sq = y.astype(F32)
        sq = sq * sq
        lane = lax.broadcasted_iota(jnp.int32, (1, LANES), 1)
        out = jnp.zeros((1, LANES), F32)
        for hh in range(y.shape[1] // HEAD_DIM):
            n2 = jnp.sum(sq[:, hh * HEAD_DIM:(hh + 1) * HEAD_DIM], axis=-1, keepdims=True)
            out = jnp.where(lane == hh, jnp.max(n2, axis=0, keepdims=True), out)
        nrm_ref[0] = out


def _proj(x_bf, wt, row_offsets, bs, n, epilogue, out_dtype, *, tm, tn, name, head_norms=False):
    m, k = x_bf.shape
    n_w = len(row_offsets)
    assert all(off % tn == 0 for off in row_offsets)
    in_specs = [pl.BlockSpec((tm, k), lambda i, j: (i, 0))]
    in_specs += [pl.BlockSpec((tn, k), functools.partial(lambda i, j, o: (o + j, 0), o=off // tn))
                 for off in row_offsets]
    in_specs += [pl.BlockSpec((1, tn), lambda i, j: (0, j)) for _ in bs]
    out_specs = [pl.BlockSpec((tm, tn), lambda i, j: (i, j))]
    out_shape = [jax.ShapeDtypeStruct((m, n), out_dtype)]
    nj = n // tn
    if head_norms:
        out_specs.append(pl.BlockSpec((1, 1, LANES), lambda i, j: (i * nj + j, 0, 0)))
        out_shape.append(jax.ShapeDtypeStruct((m // tm * nj, 1, LANES), F32))
    res = pl.pallas_call(
        functools.partial(_proj_kernel, n_w=n_w, epilogue=epilogue, head_norms=head_norms, n_split=2),
        grid=(m // tm, nj), in_specs=in_specs, out_specs=out_specs, out_shape=out_shape,
        compiler_params=_cparams(("parallel", "arbitrary"), 56),
        name=name,
    )(x_bf, *([wt] * n_w), *bs)
    if not head_norms:
        return res[0]
    heads_per_tile = tn // HEAD_DIM
    norms = res[1].reshape(m // tm, nj, LANES)[:, :, :heads_per_tile].reshape(m // tm, n // HEAD_DIM)
    return res[0], norms


def _proj_t_kernel(w_ref, x_ref, b_ref, o_ref, wbf_ref):
    @pl.when(pl.program_id(1) == 0)
    def _():
        wbf_ref[...] = w_ref[...].astype(BF16)

    z = lax.dot_general(wbf_ref[...], x_ref[...], _NT, preferred_element_type=F32)
    o_ref[...] = (z + b_ref[...]).astype(o_ref.dtype)


def _proj_t(x_bf, wt, row_offset, b_col, n, *, ts, tn, name):
    m, k = x_bf.shape
    assert row_offset % tn == 0
    return pl.pallas_call(
        _proj_t_kernel,
        grid=(n // tn, m // ts),
        in_specs=[pl.BlockSpec((tn, k), lambda a, i: (row_offset // tn + a, 0)),
                  pl.BlockSpec((ts, k), lambda a, i: (i, 0)),
                  pl.BlockSpec((tn, 1), lambda a, i: (a, 0))],
        out_specs=pl.BlockSpec((None, tn, ts), lambda a, i: (i, a, 0)),
        out_shape=jax.ShapeDtypeStruct((m // ts, n, ts), BF16),
        scratch_shapes=[pltpu.VMEM((tn, k), BF16)],
        compiler_params=_cparams(("parallel", "arbitrary"), 48),
        name=name,
    )(wt, x_bf, b_col)


_FIELD = N_HEADS


def _split3(c):
    hi = c.astype(BF16).astype(F32)
    r1 = c - hi
    mid = r1.astype(BF16).astype(F32)
    lo = (r1 - mid).astype(BF16).astype(F32)
    return hi, mid, lo


def _cum_kernel(x_ref, wf_ref, bf_ref, xbf_ref, kaug_ref, qall_ref, cfirst_ref, clast_ref, carry_ref, *, tb):
    @pl.when(pl.program_id(0) == 0)
    def _():
        carry_ref[...] = jnp.zeros_like(carry_ref)

    x_bf = x_ref[...].astype(BF16)
    xbf_ref[...] = x_bf
    f = lax.dot_general(x_bf, wf_ref[...], _NT, preferred_element_type=F32) + bf_ref[...]
    log_f = jnp.minimum(f, 0.0) - jnp.log1p(jnp.exp(-jnp.abs(f)))
    row = lax.broadcasted_iota(jnp.int32, (tb, tb), 0)
    col = lax.broadcasted_iota(jnp.int32, (tb, tb), 1)
    lower = (col <= row).astype(F32)
    cum = jnp.dot(lower, log_f, preferred_element_type=F32,
                  precision=lax.Precision.HIGHEST) + carry_ref[...]
    carry_ref[...] = cum[tb - 1:tb, :]

    cum2 = cum * LOG2E
    cfirst_ref[0] = cum2[0:1, :]
    clast_ref[0] = cum2[tb - 1:tb, :]
    hi, mid, lo = _split3(cum2)
    lane = lax.broadcasted_iota(jnp.int32, (tb, LANES), 1)
    field = lambda v, n: pltpu.roll(v, n * _FIELD, axis=1)
    k_cols = jnp.where(lane < _FIELD, -hi,
             jnp.where(lane < 2 * _FIELD, -field(mid, 1),
             jnp.where(lane < 3 * _FIELD, -field(lo, 2),
             jnp.where(lane < 6 * _FIELD, 1.0, 0.0))))
    q_cols = jnp.where(lane < 3 * _FIELD, 1.0,
             jnp.where(lane < 4 * _FIELD, field(hi, 3),
             jnp.where(lane < 5 * _FIELD, field(mid, 4),
             jnp.where(lane < 6 * _FIELD, field(lo, 5), 0.0))))
    kaug_ref[...] = k_cols.astype(BF16)
    qall_ref[...] = q_cols.astype(BF16)


def _cum(x2, wf, bf, *, tb):
    m, k = x2.shape
    aug_shape = jax.ShapeDtypeStruct((m, LANES), BF16)
    aug_spec = pl.BlockSpec((tb, LANES), lambda i: (i, 0))
    edge_shape = jax.ShapeDtypeStruct((m // tb, 1, LANES), F32)
    edge_spec = pl.BlockSpec((1, 1, LANES), lambda i: (i, 0, 0))
    return pl.pallas_call(
        functools.partial(_cum_kernel, tb=tb),
        grid=(m // tb,),
        in_specs=[pl.BlockSpec((tb, k), lambda i: (i, 0)),
                  pl.BlockSpec((LANES, k), lambda i: (0, 0)),
                  pl.BlockSpec((1, LANES), lambda i: (0, 0))],
        out_specs=[pl.BlockSpec((tb, k), lambda i: (i, 0)), aug_spec, aug_spec, edge_spec, edge_spec],
        out_shape=[jax.ShapeDtypeStruct((m, k), BF16), aug_shape, aug_shape, edge_shape, edge_shape],
        scratch_shapes=[pltpu.VMEM((1, LANES), F32)],
        compiler_params=_cparams(("arbitrary",), 32),
        name="forget_cumsum",
    )(x2, wf, bf)


def _plan_kernel(qn2_ref, kn2_ref, cfirst_ref, clast_ref, o_ref, *, nq, q_per_norm_block, kv_per_q):
    kn = jnp.sqrt(jnp.max(kn2_ref[...], axis=0, keepdims=True))
    clast = clast_ref[...]
    n_kv = clast.shape[0]
    blk = lax.broadcasted_iota(jnp.int32, (n_kv, LANES), 0)
    assert kv_per_q == 2
    pair_shift = jnp.full((1, LANES), 1, jnp.int32)
    for i in range(nq):
        qn = jnp.sqrt(qn2_ref[i // q_per_norm_block:i // q_per_norm_block + 1, :])
        bound = (2.0 * NORM_INFLATE) * qn * kn + (cfirst_ref[kv_per_q * i:kv_per_q * i + 1, :] - clast)
        skip = (bound < -(F32_EXP2_UNDERFLOW + SKIP_SLACK)) & (blk < kv_per_q * i)
        first = jnp.min(jnp.where(skip, n_kv, blk), axis=0, keepdims=True)
        o_ref[i:i + 1, :] = lax.shift_right_logical(first, pair_shift)


def _plan(qn2, kn2, cfirst, clast, *, nq, q_per_norm_block, kv_per_q):
    full = lambda a: pl.BlockSpec(a.shape, lambda: (0,) * a.ndim)
    args = (qn2, kn2, cfirst, clast)
    return pl.pallas_call(
        functools.partial(_plan_kernel, nq=nq, q_per_norm_block=q_per_norm_block, kv_per_q=kv_per_q),
        in_specs=[full(a) for a in args],
        out_specs=pl.BlockSpec((nq, LANES), lambda: (0, 0)),
        out_shape=jax.ShapeDtypeStruct((nq, LANES), jnp.int32),
        name="attention_plan",
    )(*args)


def _attn_kernel(first_ref, q_ref, qall_ref, k_ref, kaug_ref, vt_ref, ga_ref, o_ref,
                 qa0_ref, qa1_ref, s0_ref, s1_ref, acc_ref, m_ref, l_ref, *, tq, tkv, nq):
    h = pl.program_id(0)
    lane = lax.broadcasted_iota(jnp.int32, (tq, LANES), 1)
    own = (lane & (_FIELD - 1)) == h

    def load_queries(i, qa_ref):
        r0 = pl.multiple_of(i * tq, tq)
        qall = qall_ref[pl.ds(r0, tq), :]
        qa_ref[...] = jnp.concatenate(
            [q_ref[pl.ds(r0, tq), :], jnp.where(own, qall, jnp.zeros_like(qall))], axis=1)

    def scores(j, qa_ref, s_ref, c0=0):
        r0 = pl.multiple_of(j * tkv, tkv)
        ka = jnp.concatenate([k_ref[pl.ds(r0, tkv), :], kaug_ref[pl.ds(r0, tkv), :]], axis=1)
        s_ref[:, c0:] = lax.dot_general(ka, qa_ref[c0:, :], _NT, preferred_element_type=F32)

    def softmax_pv(j, s_ref, c0=0, c1=tq, key_offset=None):
        s = s_ref[:, c0:c1]
        if key_offset is not None:
            key_pos = lax.broadcasted_iota(jnp.int32, s.shape, 0) + key_offset
            qry_pos = lax.broadcasted_iota(jnp.int32, s.shape, 1) + c0
            s = jnp.where(key_pos <= qry_pos, s, -jnp.inf)
        m_prev = m_ref[:, c0:c1]
        m_new = jnp.maximum(m_prev, jnp.max(s, axis=0, keepdims=True))
        alpha = jnp.exp2(m_prev - m_new)
        p = jnp.exp2(s - m_new)
        l_ref[:, c0:c1] = alpha * l_ref[:, c0:c1] + jnp.sum(p, axis=0, keepdims=True)
        pv = jnp.dot(vt_ref[j], p.astype(BF16), preferred_element_type=F32)
        acc_ref[:, c0:c1] = alpha * acc_ref[:, c0:c1] + pv
        m_ref[:, c0:c1] = m_new

    def first_pair(i):
        return first_ref[h * nq + i]

    def query_block(i, qa_ref, qa_next_ref):
        def pair(p, carry):
            j = 2 * p
            scores(j + 1, qa_ref, s1_ref)
            softmax_pv(j, s0_ref)
            scores(j + 2, qa_ref, s0_ref)
            softmax_pv(j + 1, s1_ref)
            return carry

        m_ref[...] = jnp.full_like(m_ref, -jnp.inf)
        l_ref[...] = jnp.zeros_like(l_ref)
        acc_ref[...] = jnp.zeros_like(acc_ref)
        lax.fori_loop(first_pair(i), i, pair, 0)
        scores(2 * i + 1, qa_ref, s1_ref, c0=tkv)
        softmax_pv(2 * i, s0_ref, 0, tkv, key_offset=0)
        softmax_pv(2 * i, s0_ref, tkv, tq)
        nxt = jnp.minimum(i + 1, nq - 1)
        load_queries(nxt, qa_next_ref)
        scores(2 * first_pair(nxt), qa_next_ref, s0_ref)
        softmax_pv(2 * i + 1, s1_ref, tkv, tq, key_offset=tkv)
        rows = pl.ds(pl.multiple_of(i * tq, tq), tq)
        o = (acc_ref[...] / l_ref[...]).T
        o_ref[rows, :] = (o * ga_ref[rows, :]).astype(o_ref.dtype)

    def two_query_blocks(t, carry):
        query_block(2 * t, qa0_ref, qa1_ref)
        query_block(2 * t + 1, qa1_ref, qa0_ref)
        return carry

    assert nq % 2 == 0
    load_queries(0, qa0_ref)
    scores(2 * first_pair(0), qa0_ref, s0_ref)
    lax.fori_loop(0, nq // 2, two_query_blocks, 0)


def _attn(first_pair, q, k, vt, qall, kaug, ga, *, tq, tkv):
    s = q.shape[0]
    assert tq == 2 * tkv and vt.shape == (s // tkv, W_ATTN, tkv)
    head = lambda h, f: (0, h)
    grid_spec = pltpu.PrefetchScalarGridSpec(
        num_scalar_prefetch=1,
        grid=(N_HEADS,),
        in_specs=[pl.BlockSpec((s, HEAD_DIM), head),
                  pl.BlockSpec((s, LANES), lambda h, f: (0, 0)),
                  pl.BlockSpec((s, HEAD_DIM), head),
                  pl.BlockSpec((s, LANES), lambda h, f: (0, 0)),
                  pl.BlockSpec((s // tkv, HEAD_DIM, tkv), lambda h, f: (0, h, 0)),
                  pl.BlockSpec((s, HEAD_DIM), head)],
        out_specs=pl.BlockSpec((s, HEAD_DIM), head),
        scratch_shapes=[pltpu.VMEM((tq, 2 * HEAD_DIM), BF16),
                        pltpu.VMEM((tq, 2 * HEAD_DIM), BF16),
                        pltpu.VMEM((tkv, tq), F32),
                        pltpu.VMEM((tkv, tq), F32),
                        pltpu.VMEM((HEAD_DIM, tq), F32),
                        pltpu.VMEM((1, tq), F32),
                        pltpu.VMEM((1, tq), F32)])
    return pl.pallas_call(
        functools.partial(_attn_kernel, tq=tq, tkv=tkv, nq=s // tq),
        grid_spec=grid_spec,
        out_shape=jax.ShapeDtypeStruct((s, W_ATTN), BF16),
        compiler_params=_cparams(("parallel",), 56),
        name="forgetting_attention",
    )(first_pair, q, qall, k, kaug, vt, ga)


_N_LC = W_CONV // LANES
_TAPS_PAD = 32


def _conv_kernel(ucur_ref, uprev_ref, gc_ref, wdw_ref, bdw_ref, g_ref, b_ref, o_ref,
                 buf_ref, c_ref, *, tr, rc):
    i = pl.program_id(0)
    lanes = lambda c: slice(c * LANES, (c + 1) * LANES)
    for c in range(_N_LC):
        buf_ref[c, 0:HALO, :] = jnp.where(i == 0, 0.0, uprev_ref[:, lanes(c)])
        buf_ref[c, HALO:HALO + tr, :] = ucur_ref[:, lanes(c)]
    lead = HALO - (CONV_WIDTH - 1)
    win_rows = rc + HALO

    def lane_chunk(c, carry):
        for r0 in range(0, tr, rc):
            win = buf_ref[c, r0:r0 + win_rows, :]
            acc = jnp.broadcast_to(bdw_ref[c], (rc, LANES))
            for b in range(SUBLANES):
                rot = win if b == 0 else pltpu.roll(win, win_rows - b, axis=0)
                for a in range(win_rows // SUBLANES):
                    j = SUBLANES * a + b - lead
                    if 0 <= j < CONV_WIDTH:
                        acc = acc + wdw_ref[c, j:j + 1, :] * rot[SUBLANES * a:SUBLANES * a + rc, :]
            c_ref[c, r0:r0 + rc, :] = acc
        return carry

    lax.fori_loop(0, _N_LC, lane_chunk, 0)

    total = c_ref[0]
    for c in range(1, _N_LC):
        total = total + c_ref[c]
    mu = jnp.sum(total, axis=-1, keepdims=True) * (1.0 / W_CONV)
    sq = jnp.zeros_like(total)
    for c in range(_N_LC):
        d = c_ref[c] - mu
        sq = sq + d * d
    rstd = lax.rsqrt(jnp.sum(sq, axis=-1, keepdims=True) * (1.0 / W_CONV) + LN_EPS)
    for c in range(_N_LC):
        y = (c_ref[c] - mu) * rstd * g_ref[:, lanes(c)] + b_ref[:, lanes(c)]
        y = y * _sigmoid(y)
        o_ref[:, lanes(c)] = (y * gc_ref[:, lanes(c)]).astype(o_ref.dtype)


def _conv(u, gc, w_dw, b_dw, g_cn, b_cn, *, tr):
    s, w = u.shape
    wdw = jnp.pad(w_dw, ((0, _TAPS_PAD - CONV_WIDTH), (0, 0))).reshape(_TAPS_PAD, _N_LC, LANES)
    wdw = wdw.transpose(1, 0, 2)
    bdw = b_dw.reshape(_N_LC, 1, LANES)
    vec = lambda: pl.BlockSpec((1, w), lambda i: (0, 0))
    return pl.pallas_call(
        functools.partial(_conv_kernel, tr=tr, rc=128),
        grid=(s // tr,),
        in_specs=[pl.BlockSpec((tr, w), lambda i: (i, 0)),
                  pl.BlockSpec((HALO, w), lambda i: (jnp.maximum(i * (tr // HALO) - 1, 0), 0)),
                  pl.BlockSpec((tr, w), lambda i: (i, 0)),
                  pl.BlockSpec((_N_LC, _TAPS_PAD, LANES), lambda i: (0, 0, 0)),
                  pl.BlockSpec((_N_LC, 1, LANES), lambda i: (0, 0, 0)),
                  vec(), vec()],
        out_specs=pl.BlockSpec((tr, w), lambda i: (i, 0)),
        out_shape=jax.ShapeDtypeStruct((s, w), BF16),
        scratch_shapes=[pltpu.VMEM((_N_LC, HALO + tr, LANES), F32), pltpu.VMEM((_N_LC, tr, LANES), F32)],
        compiler_params=_cparams(("parallel",), 40),
        name="conformer_conv",
    )(u, u, gc, wdw, bdw, g_cn, b_cn)


def _out_kernel(yc_ref, ya_ref, w_ref, x_ref, bo_ref, g_ref, b_ref, o_ref, *, n_split):
    rows_per = o_ref.shape[0] // n_split
    for r in range(n_split):
        rows = slice(r * rows_per, (r + 1) * rows_per)
        sub = jnp.dot(yc_ref[rows, :], w_ref[0:W_CONV, :], preferred_element_type=F32)
        sub = sub + jnp.dot(ya_ref[rows, :], w_ref[W_CONV:W_CONV + W_ATTN, :], preferred_element_type=F32)
        h = DN_ALPHA * x_ref[rows, :] + (sub + bo_ref[...])
        o_ref[rows, :] = _layer_norm(h, g_ref[...], b_ref[...])


def _out(yc, ya, w_out_bf, x2, b_out, g_post, b_post, *, tm):
    s, d = x2.shape
    e = w_out_bf.shape[0]
    vec = lambda: pl.BlockSpec((1, d), lambda i: (0, 0))
    return pl.pallas_call(
        functools.partial(_out_kernel, n_split=2),
        grid=(s // tm,),
        in_specs=[pl.BlockSpec((tm, W_CONV), lambda i: (i, 0)),
                  pl.BlockSpec((tm, W_ATTN), lambda i: (i, 0)),
                  pl.BlockSpec((e, d), lambda i: (0, 0), pipeline_mode=pl.Buffered(1)),
                  pl.BlockSpec((tm, d), lambda i: (i, 0)),
                  vec(), vec(), vec()],
        out_specs=pl.BlockSpec((tm, d), lambda i: (i, 0)),
        out_shape=jax.ShapeDtypeStruct((s, d), F32),
        compiler_params=_cparams(("parallel",), 56),
        name="out_proj_deepnorm",
    )(yc, ya, w_out_bf, x2, b_out, g_post, b_post)


def kernel(x, w_in, b_in, w_dw, b_dw, g_conv_norm, b_conv_norm, w_out, b_out, g_post, b_post):
    assert x.shape == (1, SEQ, D_MODEL)
    x2 = x[0]
    row = lambda v: v.reshape(1, -1)
    bias = lambda a, n: row(b_in[a:a + n])
    lane_pad = lambda a: jnp.pad(a, ((0, 0), (0, LANES - a.shape[1])))

    o_val, o_gate, o_gc, o_q, o_k, o_v, o_f = 0, 2048, 4096, 6144, 8192, 10240, 12288
    o_ga = o_f + N_HEADS
    wt = w_in.T
    wt_ga = wt[o_ga:o_ga + W_ATTN]
    wt_f = jnp.pad(wt[o_f:o_ga], ((0, LANES - N_HEADS), (0, 0))).astype(BF16)
    b_f = lane_pad(bias(o_f, N_HEADS))

    tq, tkv = 1024, 512
    tm, tn = 2048, 512
    q_scale = LOG2E / math.sqrt(HEAD_DIM)
    silu = lambda z: z * _sigmoid(z)
    x_bf, kaug, qall, cfirst, clast = _cum(x2, wt_f, b_f, tb=tkv)
    u = _proj(x_bf, wt, [o_val, o_gate], [bias(o_val, W_CONV), bias(o_gate, W_CONV)], W_CONV,
              lambda a, g: a * _sigmoid(g), F32, tm=tm // 2, tn=tn, name="proj_glu")
    gc = _proj(x_bf, wt, [o_gc], [bias(o_gc, W_CONV)], W_CONV, silu, F32,
               tm=tm, tn=tn, name="proj_gate_conv")
    q, qn2 = _proj(x_bf, wt, [o_q], [bias(o_q, W_ATTN)], W_ATTN, lambda z: z * q_scale, BF16,
                   tm=tm, tn=tn, name="proj_q", head_norms=True)
    k, kn2 = _proj(x_bf, wt, [o_k], [bias(o_k, W_ATTN)], W_ATTN, lambda z: z, BF16,
                   tm=tm, tn=tn, name="proj_k", head_norms=True)
    vt = _proj_t(x_bf, wt, o_v, b_in[o_v:o_v + W_ATTN].reshape(-1, 1), W_ATTN,
                 ts=tkv, tn=1024, name="proj_v_transposed")
    ga = _proj(x_bf, wt_ga, [0], [bias(o_ga, W_ATTN)], W_ATTN, silu, F32,
               tm=tm, tn=tn, name="proj_gate_attn")

    first_pair = _plan(lane_pad(qn2), lane_pad(kn2), cfirst[:, 0], clast[:, 0],
                       nq=SEQ // tq, q_per_norm_block=tm // tq, kv_per_q=tq // tkv)
    first_pair = first_pair[:, :N_HEADS].T.reshape(-1)

    y_conv = _conv(u, gc, w_dw, row(b_dw), row(g_conv_norm), row(b_conv_norm), tr=256)
    y_attn = _attn(first_pair, q, k, vt, qall, kaug, ga, tq=tq, tkv=tkv)
    out = _out(y_conv, y_attn, w_out.astype(BF16), x2, row(b_out), row(g_post), row(b_post), tm=512)
    return out[None]
```

```python
import functools
import math

import jax
import jax.numpy as jnp
from jax import lax
from jax.experimental import pallas as pl
from jax.experimental.pallas import tpu as pltpu

D_MODEL = 2048
SEQ = 8192
W_CONV = 2048
W_ATTN = 2048
HEAD_DIM = 128
N_HEADS = W_ATTN // HEAD_DIM
CONV_WIDTH = 31
LN_EPS = 1e-5
DN_ALPHA = 2.0 ** 0.25
LOG2E = math.log2(math.e)
LANES = 128
SUBLANES = 8
HALO = 32
MIB = 1024 * 1024

F32_EXP2_UNDERFLOW = 152.0
SKIP_SLACK = 2.0
NORM_INFLATE = 1.02

F32 = jnp.float32
BF16 = jnp.bfloat16

_NT = (((1,), (1,)), ((), ()))


def _cparams(semantics, vmem_mib):
    return pltpu.CompilerParams(dimension_semantics=semantics,
                                vmem_limit_bytes=vmem_mib * MIB)


def _sigmoid(x):
    return 1.0 / (1.0 + jnp.exp(-x))


def _layer_norm(h, g, b):
    mu = jnp.mean(h, axis=-1, keepdims=True)
    d = h - mu
    var = jnp.mean(d * d, axis=-1, keepdims=True)
    return d * lax.rsqrt(var + LN_EPS) * g + b


def _proj_kernel(x_ref, *refs, n_w, epilogue, head_norms, n_split):
    w_refs, b_refs, o_ref = refs[:n_w], refs[n_w:2 * n_w], refs[2 * n_w]
    ws = [w[...].astype(BF16) for w in w_refs]
    rows_per = o_ref.shape[0] // n_split
    for r in range(n_split):
        rows = slice(r * rows_per, (r + 1) * rows_per)
        zs = [lax.dot_general(x_ref[rows, :], w, _NT, preferred_element_type=F32) + b[...]
              for w, b in zip(ws, b_refs)]
        o_ref[rows, :] = epilogue(*zs).astype(o_ref.dtype)
    if head_norms:
        y = o_ref[...]
        nrm_ref = refs[2 * n_w + 1]
        sq = y.astype(F32)
        sq = sq * sq
        lane = lax.broadcasted_iota(jnp.int32, (1, LANES), 1)
        out = jnp.zeros((1, LANES), F32)
        for hh in range(y.shape[1] // HEAD_DIM):
            n2 = jnp.sum(sq[:, hh * HEAD_DIM:(hh + 1) * HEAD_DIM], axis=-1, keepdims=True)
            out = jnp.where(lane == hh, jnp.max(n2, axis=0, keepdims=True), out)
        nrm_ref[0] = out


def _proj(x_bf, wt, row_offsets, bs, n, epilogue, out_dtype, *, tm, tn, name, head_norms=False):
    m, k = x_bf.shape
    n_w = len(row_offsets)
    assert all(off % tn == 0 for off in row_offsets)
    in_specs = [pl.BlockSpec((tm, k), lambda i, j: (i, 0))]
    in_specs += [pl.BlockSpec((tn, k), functools.partial(lambda i, j, o: (o + j, 0), o=off // tn))
                 for off in row_offsets]
    in_specs += [pl.BlockSpec((1, tn), lambda i, j: (0, j)) for _ in bs]
    out_specs = [pl.BlockSpec((tm, tn), lambda i, j: (i, j))]
    out_shape = [jax.ShapeDtypeStruct((m, n), out_dtype)]
    nj = n // tn
    if head_norms:
        out_specs.append(pl.BlockSpec((1, 1, LANES), lambda i, j: (i * nj + j, 0, 0)))
        out_shape.append(jax.ShapeDtypeStruct((m // tm * nj, 1, LANES), F32))
    res = pl.pallas_call(
        functools.partial(_proj_kernel, n_w=n_w, epilogue=epilogue, head_norms=head_norms, n_split=2),
        grid=(m // tm, nj), in_specs=in_specs, out_specs=out_specs, out_shape=out_shape,
        compiler_params=_cparams(("parallel", "arbitrary"), 56),
        name=name,
    )(x_bf, *([wt] * n_w), *bs)
    if not head_norms:
        return res[0]
    heads_per_tile = tn // HEAD_DIM
    norms = res[1].reshape(m // tm, nj, LANES)[:, :, :heads_per_tile].reshape(m // tm, n // HEAD_DIM)
    return res[0], norms


def _proj_t_kernel(w_ref, x_ref, b_ref, o_ref, wbf_ref):
    @pl.when(pl.program_id(1) == 0)
    def _():
        wbf_ref[...] = w_ref[...].astype(BF16)

    ts = o_ref.shape[2]
    for r in range(o_ref.shape[0]):
        z = lax.dot_general(wbf_ref[...], x_ref[r * ts:(r + 1) * ts, :], _NT,
                            preferred_element_type=F32)
        o_ref[r] = (z + b_ref[...]).astype(o_ref.dtype)


def _proj_t(x_bf, wt, row_offset, b_col, n, *, ts, tn, blocks_per_step, name):
    m, k = x_bf.shape
    assert row_offset % tn == 0
    bps = blocks_per_step
    return pl.pallas_call(
        _proj_t_kernel,
        grid=(n // tn, m // (ts * bps)),
        in_specs=[pl.BlockSpec((tn, k), lambda a, i: (row_offset // tn + a, 0)),
                  pl.BlockSpec((ts * bps, k), lambda a, i: (i, 0)),
                  pl.BlockSpec((tn, 1), lambda a, i: (a, 0))],
        out_specs=pl.BlockSpec((bps, tn, ts), lambda a, i: (i, a, 0)),
        out_shape=jax.ShapeDtypeStruct((m // ts, n, ts), BF16),
        scratch_shapes=[pltpu.VMEM((tn, k), BF16)],
        compiler_params=_cparams(("parallel", "arbitrary"), 48),
        name=name,
    )(wt, x_bf, b_col)


_FIELD = N_HEADS


def _split3(c):
    hi = c.astype(BF16).astype(F32)
    r1 = c - hi
    mid = r1.astype(BF16).astype(F32)
    lo = (r1 - mid).astype(BF16).astype(F32)
    return hi, mid, lo


def _cum_kernel(x_ref, wf_ref, bf_ref, xbf_ref, kaug_ref, qall_ref, cfirst_ref, clast_ref, carry_ref, *, tb):
    @pl.when(pl.program_id(0) == 0)
    def _():
        carry_ref[...] = jnp.zeros_like(carry_ref)

    x_bf = x_ref[...].astype(BF16)
    xbf_ref[...] = x_bf
    f = lax.dot_general(x_bf, wf_ref[...], _NT, preferred_element_type=F32) + bf_ref[...]
    log_f = jnp.minimum(f, 0.0) - jnp.log1p(jnp.exp(-jnp.abs(f)))
    row = lax.broadcasted_iota(jnp.int32, (tb, tb), 0)
    col = lax.broadcasted_iota(jnp.int32, (tb, tb), 1)
    lower = (col <= row).astype(BF16)
    parts = jnp.concatenate(_split3(log_f), axis=1).astype(BF16)
    sums = jnp.dot(lower, parts, preferred_element_type=F32)
    cum = (sums[:, :LANES] + sums[:, LANES:2 * LANES]) + sums[:, 2 * LANES:] + carry_ref[...]
    carry_ref[...] = cum[tb - 1:tb, :]

    cum2 = cum * LOG2E
    cfirst_ref[0] = cum2[0:1, :]
    clast_ref[0] = cum2[tb - 1:tb, :]
    hi, mid, lo = _split3(cum2)
    lane = lax.broadcasted_iota(jnp.int32, (tb, LANES), 1)
    field = lambda v, n: pltpu.roll(v, n * _FIELD, axis=1)
    k_cols = jnp.where(lane < _FIELD, -hi,
             jnp.where(lane < 2 * _FIELD, -field(mid, 1),
             jnp.where(lane < 3 * _FIELD, -field(lo, 2),
             jnp.where(lane < 6 * _FIELD, 1.0, 0.0))))
    q_cols = jnp.where(lane < 3 * _FIELD, 1.0,
             jnp.where(lane < 4 * _FIELD, field(hi, 3),
             jnp.where(lane < 5 * _FIELD, field(mid, 4),
             jnp.where(lane < 6 * _FIELD, field(lo, 5), 0.0))))
    kaug_ref[...] = k_cols.astype(BF16)
    qall_ref[...] = q_cols.astype(BF16)


def _cum(x2, wf, bf, *, tb):
    m, k = x2.shape
    aug_shape = jax.ShapeDtypeStruct((m, LANES), BF16)
    aug_spec = pl.BlockSpec((tb, LANES), lambda i: (i, 0))
    edge_shape = jax.ShapeDtypeStruct((m // tb, 1, LANES), F32)
    edge_spec = pl.BlockSpec((1, 1, LANES), lambda i: (i, 0, 0))
    return pl.pallas_call(
        functools.partial(_cum_kernel, tb=tb),
        grid=(m // tb,),
        in_specs=[pl.BlockSpec((tb, k), lambda i: (i, 0)),
                  pl.BlockSpec((LANES, k), lambda i: (0, 0)),
                  pl.BlockSpec((1, LANES), lambda i: (0, 0))],
        out_specs=[pl.BlockSpec((tb, k), lambda i: (i, 0)), aug_spec, aug_spec, edge_spec, edge_spec],
        out_shape=[jax.ShapeDtypeStruct((m, k), BF16), aug_shape, aug_shape, edge_shape, edge_shape],
        scratch_shapes=[pltpu.VMEM((1, LANES), F32)],
        compiler_params=_cparams(("arbitrary",), 32),
        name="forget_cumsum",
    )(x2, wf, bf)


def _plan_kernel(qn2_ref, kn2_ref, cfirst_ref, clast_ref, o_ref, *, nq, q_per_norm_block, kv_per_q):
    kn = jnp.sqrt(jnp.max(kn2_ref[...], axis=0, keepdims=True))
    clast = clast_ref[...]
    n_kv = clast.shape[0]
    blk = lax.broadcasted_iota(jnp.int32, (n_kv, LANES), 0)
    assert kv_per_q == 2
    pair_shift = jnp.full((1, LANES), 1, jnp.int32)
    for i in range(nq):
        qn = jnp.sqrt(qn2_ref[i // q_per_norm_block:i // q_per_norm_block + 1, :])
        bound = (2.0 * NORM_INFLATE) * qn * kn + (cfirst_ref[kv_per_q * i:kv_per_q * i + 1, :] - clast)
        skip = (bound < -(F32_EXP2_UNDERFLOW + SKIP_SLACK)) & (blk < kv_per_q * i)
        first = jnp.min(jnp.where(skip, n_kv, blk), axis=0, keepdims=True)
        o_ref[i:i + 1, :] = lax.shift_right_logical(first, pair_shift)


def _plan(qn2, kn2, cfirst, clast, *, nq, q_per_norm_block, kv_per_q):
    full = lambda a: pl.BlockSpec(a.shape, lambda: (0,) * a.ndim)
    args = (qn2, kn2, cfirst, clast)
    return pl.pallas_call(
        functools.partial(_plan_kernel, nq=nq, q_per_norm_block=q_per_norm_block, kv_per_q=kv_per_q),
        in_specs=[full(a) for a in args],
        out_specs=pl.BlockSpec((nq, LANES), lambda: (0, 0)),
        out_shape=jax.ShapeDtypeStruct((nq, LANES), jnp.int32),
        name="attention_plan",
    )(*args)


def _attn_kernel(first_ref, q_ref, qall_ref, k_ref, kaug_ref, vt_ref, ga_ref, o_ref,
                 qa0_ref, qa1_ref, s0_ref, s1_ref, acc_ref, m_ref, l_ref, *, tq, tkv, nq):
    h = pl.program_id(0)
    lane = lax.broadcasted_iota(jnp.int32, (tq, LANES), 1)
    own = (lane & (_FIELD - 1)) == h

    def load_queries(i, qa_ref):
        r0 = pl.multiple_of(i * tq, tq)
        qall = qall_ref[pl.ds(r0, tq), :]
        qa_ref[...] = jnp.concatenate(
            [q_ref[pl.ds(r0, tq), :], jnp.where(own, qall, jnp.zeros_like(qall))], axis=1)

    def scores(j, qa_ref, s_ref, c0=0):
        r0 = pl.multiple_of(j * tkv, tkv)
        ka = jnp.concatenate([k_ref[pl.ds(r0, tkv), :], kaug_ref[pl.ds(r0, tkv), :]], axis=1)
        s_ref[:, c0:] = lax.dot_general(ka, qa_ref[c0:, :], _NT, preferred_element_type=F32)

    def softmax_pv(j, s_ref, c0=0, c1=tq, key_offset=None):
        s = s_ref[:, c0:c1]
        if key_offset is not None:
            key_pos = lax.broadcasted_iota(jnp.int32, s.shape, 0) + key_offset
            qry_pos = lax.broadcasted_iota(jnp.int32, s.shape, 1) + c0
            s = jnp.where(key_pos <= qry_pos, s, -jnp.inf)
        m_prev = m_ref[:, c0:c1]
        m_new = jnp.maximum(m_prev, jnp.max(s, axis=0, keepdims=True))
        alpha = jnp.exp2(m_prev - m_new)
        p = jnp.exp2(s - m_new)
        l_ref[:, c0:c1] = alpha * l_ref[:, c0:c1] + jnp.sum(p, axis=0, keepdims=True)
        pv = jnp.dot(vt_ref[j], p.astype(BF16), preferred_element_type=F32)
        acc_ref[:, c0:c1] = alpha * acc_ref[:, c0:c1] + pv
        m_ref[:, c0:c1] = m_new

    def first_pair(i):
        return first_ref[h * nq + i]

    def query_block(i, qa_ref, qa_next_ref):
        def pair(p, carry):
            j = 2 * p
            scores(j + 1, qa_ref, s1_ref)
            softmax_pv(j, s0_ref)
            scores(j + 2, qa_ref, s0_ref)
            softmax_pv(j + 1, s1_ref)
            return carry

        m_ref[...] = jnp.full_like(m_ref, -jnp.inf)
        l_ref[...] = jnp.zeros_like(l_ref)
        acc_ref[...] = jnp.zeros_like(acc_ref)
        lax.fori_loop(first_pair(i), i, pair, 0)
        scores(2 * i + 1, qa_ref, s1_ref, c0=tkv)
        softmax_pv(2 * i, s0_ref, 0, tkv, key_offset=0)
        softmax_pv(2 * i, s0_ref, tkv, tq)
        nxt = jnp.minimum(i + 1, nq - 1)
        load_queries(nxt, qa_next_ref)
        scores(2 * first_pair(nxt), qa_next_ref, s0_ref)
        softmax_pv(2 * i + 1, s1_ref, tkv, tq, key_offset=tkv)
        rows = pl.ds(pl.multiple_of(i * tq, tq), tq)
        o = (acc_ref[...] / l_ref[...]).T
        o_ref[rows, :] = (o * ga_ref[rows, :]).astype(o_ref.dtype)

    def two_query_blocks(t, carry):
        query_block(2 * t, qa0_ref, qa1_ref)
        query_block(2 * t + 1, qa1_ref, qa0_ref)
        return carry

    assert nq % 2 == 0
    load_queries(0, qa0_ref)
    scores(2 * first_pair(0), qa0_ref, s0_ref)
    lax.fori_loop(0, nq // 2, two_query_blocks, 0)


def _attn(first_pair, q, k, vt, qall, kaug, ga, *, tq, tkv):
    s = q.shape[0]
    assert tq == 2 * tkv and vt.shape == (s // tkv, W_ATTN, tkv)
    head = lambda h, f: (0, h)
    grid_spec = pltpu.PrefetchScalarGridSpec(
        num_scalar_prefetch=1,
        grid=(N_HEADS,),
        in_specs=[pl.BlockSpec((s, HEAD_DIM), head),
                  pl.BlockSpec((s, LANES), lambda h, f: (0, 0)),
                  pl.BlockSpec((s, HEAD_DIM), head),
                  pl.BlockSpec((s, LANES), lambda h, f: (0, 0)),
                  pl.BlockSpec((s // tkv, HEAD_DIM, tkv), lambda h, f: (0, h, 0)),
                  pl.BlockSpec((s, HEAD_DIM), head)],
        out_specs=pl.BlockSpec((s, HEAD_DIM), head),
        scratch_shapes=[pltpu.VMEM((tq, 2 * HEAD_DIM), BF16),
                        pltpu.VMEM((tq, 2 * HEAD_DIM), BF16),
                        pltpu.VMEM((tkv, tq), F32),
                        pltpu.VMEM((tkv, tq), F32),
                        pltpu.VMEM((HEAD_DIM, tq), F32),
                        pltpu.VMEM((1, tq), F32),
                        pltpu.VMEM((1, tq), F32)])
    return pl.pallas_call(
        functools.partial(_attn_kernel, tq=tq, tkv=tkv, nq=s // tq),
        grid_spec=grid_spec,
        out_shape=jax.ShapeDtypeStruct((s, W_ATTN), BF16),
        compiler_params=_cparams(("parallel",), 56),
        name="forgetting_attention",
    )(first_pair, q, qall, k, kaug, vt, ga)


_N_LC = W_CONV // LANES
_TAPS_PAD = 32


def _conv_kernel(ucur_ref, uprev_ref, gc_ref, wdw_ref, bdw_ref, g_ref, b_ref, wo_ref, o_ref, wobf_ref,
                 buf_ref, c_ref, *, tr, rc):
    i = pl.program_id(0)
    wobf_ref[...] = wo_ref[...].astype(BF16)
    lanes = lambda c: slice(c * LANES, (c + 1) * LANES)
    for c in range(_N_LC):
        buf_ref[c, 0:HALO, :] = jnp.where(i == 0, 0.0, uprev_ref[:, lanes(c)])
        buf_ref[c, HALO:HALO + tr, :] = ucur_ref[:, lanes(c)]
    lead = HALO - (CONV_WIDTH - 1)
    win_rows = rc + HALO

    def lane_chunk(c, carry):
        for r0 in range(0, tr, rc):
            win = buf_ref[c, r0:r0 + win_rows, :]
            acc = jnp.broadcast_to(bdw_ref[c], (rc, LANES))
            for b in range(SUBLANES):
                rot = win if b == 0 else pltpu.roll(win, win_rows - b, axis=0)
                for a in range(win_rows // SUBLANES):
                    j = SUBLANES * a + b - lead
                    if 0 <= j < CONV_WIDTH:
                        acc = acc + wdw_ref[c, j:j + 1, :] * rot[SUBLANES * a:SUBLANES * a + rc, :]
            c_ref[c, r0:r0 + rc, :] = acc
        return carry

    lax.fori_loop(0, _N_LC, lane_chunk, 0)

    total = c_ref[0]
    for c in range(1, _N_LC):
        total = total + c_ref[c]
    mu = jnp.sum(total, axis=-1, keepdims=True) * (1.0 / W_CONV)
    sq = jnp.zeros_like(total)
    for c in range(_N_LC):
        d = c_ref[c] - mu
        sq = sq + d * d
    rstd = lax.rsqrt(jnp.sum(sq, axis=-1, keepdims=True) * (1.0 / W_CONV) + LN_EPS)
    for c in range(_N_LC):
        y = (c_ref[c] - mu) * rstd * g_ref[:, lanes(c)] + b_ref[:, lanes(c)]
        y = y * _sigmoid(y)
        o_ref[:, lanes(c)] = (y * gc_ref[:, lanes(c)]).astype(o_ref.dtype)


def _conv(u, gc, w_dw, b_dw, g_cn, b_cn, w_out, *, tr):
    s, w = u.shape
    e, d = w_out.shape
    slab = e // (s // tr)
    assert slab * (s // tr) == e and slab % (2 * SUBLANES) == 0
    wdw = jnp.pad(w_dw, ((0, _TAPS_PAD - CONV_WIDTH), (0, 0))).reshape(_TAPS_PAD, _N_LC, LANES)
    wdw = wdw.transpose(1, 0, 2)
    bdw = b_dw.reshape(_N_LC, 1, LANES)
    vec = lambda: pl.BlockSpec((1, w), lambda i: (0, 0))
    return pl.pallas_call(
        functools.partial(_conv_kernel, tr=tr, rc=128),
        grid=(s // tr,),
        in_specs=[pl.BlockSpec((tr, w), lambda i: (i, 0)),
                  pl.BlockSpec((HALO, w), lambda i: (jnp.maximum(i * (tr // HALO) - 1, 0), 0)),
                  pl.BlockSpec((tr, w), lambda i: (i, 0)),
                  pl.BlockSpec((_N_LC, _TAPS_PAD, LANES), lambda i: (0, 0, 0)),
                  pl.BlockSpec((_N_LC, 1, LANES), lambda i: (0, 0, 0)),
                  vec(), vec(),
                  pl.BlockSpec((slab, d), lambda i: (i, 0))],
        out_specs=[pl.BlockSpec((tr, w), lambda i: (i, 0)),
                   pl.BlockSpec((slab, d), lambda i: (i, 0))],
        out_shape=[jax.ShapeDtypeStruct((s, w), BF16),
                   jax.ShapeDtypeStruct((e, d), BF16)],
        scratch_shapes=[pltpu.VMEM((_N_LC, HALO + tr, LANES), F32), pltpu.VMEM((_N_LC, tr, LANES), F32)],
        compiler_params=_cparams(("parallel",), 40),
        name="conformer_conv",
    )(u, u, gc, wdw, bdw, g_cn, b_cn, w_out)


def _out_kernel(yc_ref, ya_ref, w_ref, x_ref, bo_ref, g_ref, b_ref, o_ref, *, n_split):
    rows_per = o_ref.shape[0] // n_split
    for r in range(n_split):
        rows = slice(r * rows_per, (r + 1) * rows_per)
        sub = jnp.dot(yc_ref[rows, :], w_ref[0:W_CONV, :], preferred_element_type=F32)
        sub = sub + jnp.dot(ya_ref[rows, :], w_ref[W_CONV:W_CONV + W_ATTN, :], preferred_element_type=F32)
        h = DN_ALPHA * x_ref[rows, :] + (sub + bo_ref[...])
        o_ref[rows, :] = _layer_norm(h, g_ref[...], b_ref[...])


def _out(yc, ya, w_out_bf, x2, b_out, g_post, b_post, *, tm):
    s, d = x2.shape
    e = w_out_bf.shape[0]
    vec = lambda: pl.BlockSpec((1, d), lambda i: (0, 0))
    return pl.pallas_call(
        functools.partial(_out_kernel, n_split=2),
        grid=(s // tm,),
        in_specs=[pl.BlockSpec((tm, W_CONV), lambda i: (i, 0)),
                  pl.BlockSpec((tm, W_ATTN), lambda i: (i, 0)),
                  pl.BlockSpec((e, d), lambda i: (0, 0), pipeline_mode=pl.Buffered(1)),
                  pl.BlockSpec((tm, d), lambda i: (i, 0)),
                  vec(), vec(), vec()],
        out_specs=pl.BlockSpec((tm, d), lambda i: (i, 0)),
        out_shape=jax.ShapeDtypeStruct((s, d), F32),
        compiler_params=_cparams(("parallel",), 56),
        name="out_proj_deepnorm",
    )(yc, ya, w_out_bf, x2, b_out, g_post, b_post)


def kernel(x, w_in, b_in, w_dw, b_dw, g_conv_norm, b_conv_norm, w_out, b_out, g_post, b_post):
    assert x.shape == (1, SEQ, D_MODEL)
    x2 = x[0]
    row = lambda v: v.reshape(1, -1)
    bias = lambda a, n: row(b_in[a:a + n])
    lane_pad = lambda a: jnp.pad(a, ((0, 0), (0, LANES - a.shape[1])))

    o_val, o_gate, o_gc, o_q, o_k, o_v, o_f = 0, 2048, 4096, 6144, 8192, 10240, 12288
    o_ga = o_f + N_HEADS
    wt = w_in.T
    wt_ga = wt[o_ga:o_ga + W_ATTN]
    wt_f = jnp.pad(wt[o_f:o_ga], ((0, LANES - N_HEADS), (0, 0))).astype(BF16)
    b_f = lane_pad(bias(o_f, N_HEADS))

    tq, tkv = 1024, 512
    tm, tn = 2048, 512
    q_scale = LOG2E / math.sqrt(HEAD_DIM)
    silu = lambda z: z * _sigmoid(z)
    x_bf, kaug, qall, cfirst, clast = _cum(x2, wt_f, b_f, tb=tkv)
    u = _proj(x_bf, wt, [o_val, o_gate], [bias(o_val, W_CONV), bias(o_gate, W_CONV)], W_CONV,
              lambda a, g: a * _sigmoid(g), F32, tm=tm // 2, tn=tn, name="proj_glu")
    gc = _proj(x_bf, wt, [o_gc], [bias(o_gc, W_CONV)], W_CONV, silu, F32,
               tm=tm, tn=tn, name="proj_gate_conv")
    q, qn2 = _proj(x_bf, wt, [o_q], [bias(o_q, W_ATTN)], W_ATTN, lambda z: z * q_scale, BF16,
                   tm=tm, tn=tn, name="proj_q", head_norms=True)
    k, kn2 = _proj(x_bf, wt, [o_k], [bias(o_k, W_ATTN)], W_ATTN, lambda z: z, BF16,
                   tm=tm, tn=tn, name="proj_k", head_norms=True)
    vt = _proj_t(x_bf, wt, o_v, b_in[o_v:o_v + W_ATTN].reshape(-1, 1), W_ATTN,
                 ts=tkv, tn=1024, blocks_per_step=2, name="proj_v_transposed")
    ga = _proj(x_bf, wt_ga, [0], [bias(o_ga, W_ATTN)], W_ATTN, silu, F32,
               tm=tm, tn=tn, name="proj_gate_attn")

    first_pair = _plan(lane_pad(qn2), lane_pad(kn2), cfirst[:, 0], clast[:, 0],
                       nq=SEQ // tq, q_per_norm_block=tm // tq, kv_per_q=tq // tkv)
    first_pair = first_pair[:, :N_HEADS].T.reshape(-1)

    y_conv, w_out_bf = _conv(u, gc, w_dw, row(b_dw), row(g_conv_norm), row(b_conv_norm), w_out, tr=256)
    y_attn = _attn(first_pair, q, k, vt, qall, kaug, ga, tq=tq, tkv=tkv)
    out = _out(y_conv, y_attn, w_out_bf, x2, row(b_out), row(g_post), row(b_post), tm=512)
    return out[None]
```

```python
import functools
import math

import jax
import jax.numpy as jnp
from jax import lax
from jax.experimental import pallas as pl
from jax.experimental.pallas import tpu as pltpu

D_MODEL = 2048
SEQ = 8192
W_CONV = 2048
W_ATTN = 2048
HEAD_DIM = 128
N_HEADS = W_ATTN // HEAD_DIM
CONV_WIDTH = 31
LN_EPS = 1e-5
DN_ALPHA = 2.0 ** 0.25
LOG2E = math.log2(math.e)
LANES = 128
SUBLANES = 8
HALO = 32
MIB = 1024 * 1024

F32_EXP2_UNDERFLOW = 152.0
SKIP_SLACK = 2.0
NORM_INFLATE = 1.02

F32 = jnp.float32
BF16 = jnp.bfloat16

_NT = (((1,), (1,)), ((), ()))


def _cparams(semantics, vmem_mib):
    return pltpu.CompilerParams(dimension_semantics=semantics,
                                vmem_limit_bytes=vmem_mib * MIB)


def _sigmoid(x):
    return 1.0 / (1.0 + jnp.exp(-x))


def _layer_norm(h, g, b):
    mu = jnp.mean(h, axis=-1, keepdims=True)
    d = h - mu
    var = jnp.mean(d * d, axis=-1, keepdims=True)
    return d * lax.rsqrt(var + LN_EPS) * g + b


def _proj_kernel(x_ref, *refs, n_w, epilogue, head_norms, n_split):
    w_refs, b_refs, o_ref = refs[:n_w], refs[n_w:2 * n_w], refs[2 * n_w]
    ws = [w[...].astype(BF16) for w in w_refs]
    rows_per = o_ref.shape[0] // n_split
    for r in range(n_split):
        rows = slice(r * rows_per, (r + 1) * rows_per)
        zs = [lax.dot_general(x_ref[rows, :], w, _NT, preferred_element_type=F32) + b[...]
              for w, b in zip(ws, b_refs)]
        o_ref[rows, :] = epilogue(*zs).astype(o_ref.dtype)
    if head_norms:
        y = o_ref[...]
        nrm_ref = refs[2 * n_w + 1]
        sq = y.astype(F32)
        sq = sq * sq
        lane = lax.broadcasted_iota(jnp.int32, (1, LANES), 1)
        out = jnp.zeros((1, LANES), F32)
        for hh in range(y.shape[1] // HEAD_DIM):
            n2 = jnp.sum(sq[:, hh * HEAD_DIM:(hh + 1) * HEAD_DIM], axis=-1, keepdims=True)
            out = jnp.where(lane == hh, jnp.max(n2, axis=0, keepdims=True), out)
        nrm_ref[0] = out


def _proj(x_bf, wt, row_offsets, bs, n, epilogue, out_dtype, *, tm, tn, name, head_norms=False):
    m, k = x_bf.shape
    n_w = len(row_offsets)
    assert all(off % SUBLANES == 0 for off in row_offsets)

    def weight_spec(off):
        if off % tn == 0:
            return pl.BlockSpec((tn, k), lambda i, j: (off // tn + j, 0))
        return pl.BlockSpec((pl.Element(tn), pl.Element(k)),
                            lambda i, j: (pl.multiple_of(off + j * tn, SUBLANES), 0))

    in_specs = [pl.BlockSpec((tm, k), lambda i, j: (i, 0))]
    in_specs += [weight_spec(off) for off in row_offsets]
    in_specs += [pl.BlockSpec((1, tn), lambda i, j: (0, j)) for _ in bs]
    out_specs = [pl.BlockSpec((tm, tn), lambda i, j: (i, j))]
    out_shape = [jax.ShapeDtypeStruct((m, n), out_dtype)]
    nj = n // tn
    if head_norms:
        out_specs.append(pl.BlockSpec((1, 1, LANES), lambda i, j: (i * nj + j, 0, 0)))
        out_shape.append(jax.ShapeDtypeStruct((m // tm * nj, 1, LANES), F32))
    res = pl.pallas_call(
        functools.partial(_proj_kernel, n_w=n_w, epilogue=epilogue, head_norms=head_norms, n_split=tm // 256),
        grid=(m // tm, nj), in_specs=in_specs, out_specs=out_specs, out_shape=out_shape,
        compiler_params=_cparams(("parallel", "arbitrary"), 56),
        name=name,
    )(x_bf, *([wt] * n_w), *bs)
    if not head_norms:
        return res[0]
    heads_per_tile = tn // HEAD_DIM
    norms = res[1].reshape(m // tm, nj, LANES)[:, :, :heads_per_tile].reshape(m // tm, n // HEAD_DIM)
    return res[0], norms


def _proj_t_kernel(w_ref, x_ref, b_ref, o_ref, wbf_ref):
    @pl.when(pl.program_id(1) == 0)
    def _():
        wbf_ref[...] = w_ref[...].astype(BF16)

    ts = o_ref.shape[2]
    for r in range(o_ref.shape[0]):
        z = lax.dot_general(wbf_ref[...], x_ref[r * ts:(r + 1) * ts, :], _NT,
                            preferred_element_type=F32)
        o_ref[r] = (z + b_ref[...]).astype(o_ref.dtype)


def _proj_t(x_bf, wt, row_offset, b_col, n, *, ts, tn, blocks_per_step, name):
    m, k = x_bf.shape
    assert row_offset % tn == 0
    bps = blocks_per_step
    return pl.pallas_call(
        _proj_t_kernel,
        grid=(n // tn, m // (ts * bps)),
        in_specs=[pl.BlockSpec((tn, k), lambda a, i: (row_offset // tn + a, 0)),
                  pl.BlockSpec((ts * bps, k), lambda a, i: (i, 0)),
                  pl.BlockSpec((tn, 1), lambda a, i: (a, 0))],
        out_specs=pl.BlockSpec((bps, tn, ts), lambda a, i: (i, a, 0)),
        out_shape=jax.ShapeDtypeStruct((m // ts, n, ts), BF16),
        scratch_shapes=[pltpu.VMEM((tn, k), BF16)],
        compiler_params=_cparams(("parallel", "arbitrary"), 48),
        name=name,
    )(wt, x_bf, b_col)


_FIELD = N_HEADS


def _split3(c):
    hi = c.astype(BF16).astype(F32)
    r1 = c - hi
    mid = r1.astype(BF16).astype(F32)
    lo = (r1 - mid).astype(BF16).astype(F32)
    return hi, mid, lo


def _cum_kernel(x_ref, wf_ref, bf_ref, xbf_ref, kaug_ref, qall_ref, cfirst_ref, clast_ref, carry_ref, *, tb):
    @pl.when(pl.program_id(0) == 0)
    def _():
        carry_ref[...] = jnp.zeros_like(carry_ref)

    x_bf = x_ref[...].astype(BF16)
    xbf_ref[...] = x_bf
    f = lax.dot_general(x_bf, wf_ref[...], _NT, preferred_element_type=F32) + bf_ref[...]
    log_f = jnp.minimum(f, 0.0) - jnp.log1p(jnp.exp(-jnp.abs(f)))
    row = lax.broadcasted_iota(jnp.int32, (tb, tb), 0)
    col = lax.broadcasted_iota(jnp.int32, (tb, tb), 1)
    lower = (col <= row).astype(BF16)
    parts = jnp.concatenate(_split3(log_f), axis=1).astype(BF16)
    sums = jnp.dot(lower, parts, preferred_element_type=F32)
    cum = (sums[:, :LANES] + sums[:, LANES:2 * LANES]) + sums[:, 2 * LANES:] + carry_ref[...]
    carry_ref[...] = cum[tb - 1:tb, :]

    cum2 = cum * LOG2E
    cfirst_ref[0] = cum2[0:1, :]
    clast_ref[0] = cum2[tb - 1:tb, :]
    hi, mid, lo = _split3(cum2)
    lane = lax.broadcasted_iota(jnp.int32, (tb, LANES), 1)
    field = lambda v, n: pltpu.roll(v, n * _FIELD, axis=1)
    k_cols = jnp.where(lane < _FIELD, -hi,
             jnp.where(lane < 2 * _FIELD, -field(mid, 1),
             jnp.where(lane < 3 * _FIELD, -field(lo, 2),
             jnp.where(lane < 6 * _FIELD, 1.0, 0.0))))
    q_cols = jnp.where(lane < 3 * _FIELD, 1.0,
             jnp.where(lane < 4 * _FIELD, field(hi, 3),
             jnp.where(lane < 5 * _FIELD, field(mid, 4),
             jnp.where(lane < 6 * _FIELD, field(lo, 5), 0.0))))
    kaug_ref[...] = k_cols.astype(BF16)
    qall_ref[...] = q_cols.astype(BF16)


def _cum(x2, wf, bf, *, tb):
    m, k = x2.shape
    aug_shape = jax.ShapeDtypeStruct((m, LANES), BF16)
    aug_spec = pl.BlockSpec((tb, LANES), lambda i: (i, 0))
    edge_shape = jax.ShapeDtypeStruct((m // tb, 1, LANES), F32)
    edge_spec = pl.BlockSpec((1, 1, LANES), lambda i: (i, 0, 0))
    return pl.pallas_call(
        functools.partial(_cum_kernel, tb=tb),
        grid=(m // tb,),
        in_specs=[pl.BlockSpec((tb, k), lambda i: (i, 0)),
                  pl.BlockSpec((LANES, k), lambda i: (0, 0)),
                  pl.BlockSpec((1, LANES), lambda i: (0, 0))],
        out_specs=[pl.BlockSpec((tb, k), lambda i: (i, 0)), aug_spec, aug_spec, edge_spec, edge_spec],
        out_shape=[jax.ShapeDtypeStruct((m, k), BF16), aug_shape, aug_shape, edge_shape, edge_shape],
        scratch_shapes=[pltpu.VMEM((1, LANES), F32)],
        compiler_params=_cparams(("arbitrary",), 32),
        name="forget_cumsum",
    )(x2, wf, bf)


def _plan_kernel(qn2_ref, kn2_ref, cfirst_ref, clast_ref, o_ref, *, nq, q_per_norm_block, kv_per_q):
    kn = jnp.sqrt(jnp.max(kn2_ref[...], axis=0, keepdims=True))
    clast = clast_ref[...]
    n_kv = clast.shape[0]
    blk = lax.broadcasted_iota(jnp.int32, (n_kv, LANES), 0)
    assert kv_per_q == 2
    pair_shift = jnp.full((1, LANES), 1, jnp.int32)
    for i in range(nq):
        qn = jnp.sqrt(qn2_ref[i // q_per_norm_block:i // q_per_norm_block + 1, :])
        bound = (2.0 * NORM_INFLATE) * qn * kn + (cfirst_ref[kv_per_q * i:kv_per_q * i + 1, :] - clast)
        skip = (bound < -(F32_EXP2_UNDERFLOW + SKIP_SLACK)) & (blk < kv_per_q * i)
        first = jnp.min(jnp.where(skip, n_kv, blk), axis=0, keepdims=True)
        o_ref[i:i + 1, :] = lax.shift_right_logical(first, pair_shift)


def _plan(qn2, kn2, cfirst, clast, *, nq, q_per_norm_block, kv_per_q):
    full = lambda a: pl.BlockSpec(a.shape, lambda: (0,) * a.ndim)
    args = (qn2, kn2, cfirst, clast)
    return pl.pallas_call(
        functools.partial(_plan_kernel, nq=nq, q_per_norm_block=q_per_norm_block, kv_per_q=kv_per_q),
        in_specs=[full(a) for a in args],
        out_specs=pl.BlockSpec((nq, LANES), lambda: (0, 0)),
        out_shape=jax.ShapeDtypeStruct((nq, LANES), jnp.int32),
        name="attention_plan",
    )(*args)


def _attn_kernel(first_ref, q_ref, qall_ref, k_ref, kaug_ref, vt_ref, ga_ref, o_ref,
                 qa0_ref, qa1_ref, s0_ref, s1_ref, acc_ref, m_ref, l_ref, *, tq, tkv, nq):
    h = pl.program_id(0)
    lane = lax.broadcasted_iota(jnp.int32, (tq, LANES), 1)
    own = (lane & (_FIELD - 1)) == h

    def load_queries(i, qa_ref):
        r0 = pl.multiple_of(i * tq, tq)
        qall = qall_ref[pl.ds(r0, tq), :]
        qa_ref[...] = jnp.concatenate(
            [q_ref[pl.ds(r0, tq), :], jnp.where(own, qall, jnp.zeros_like(qall))], axis=1)

    def scores(j, qa_ref, s_ref, c0=0):
        r0 = pl.multiple_of(j * tkv, tkv)
        ka = jnp.concatenate([k_ref[pl.ds(r0, tkv), :], kaug_ref[pl.ds(r0, tkv), :]], axis=1)
        s_ref[:, c0:] = lax.dot_general(ka, qa_ref[c0:, :], _NT, preferred_element_type=F32)

    def softmax_pv(j, s_ref, c0=0, c1=tq, key_offset=None):
        s = s_ref[:, c0:c1]
        if key_offset is not None:
            key_pos = lax.broadcasted_iota(jnp.int32, s.shape, 0) + key_offset
            qry_pos = lax.broadcasted_iota(jnp.int32, s.shape, 1) + c0
            s = jnp.where(key_pos <= qry_pos, s, -jnp.inf)
        m_prev = m_ref[:, c0:c1]
        m_new = jnp.maximum(m_prev, jnp.max(s, axis=0, keepdims=True))
        alpha = jnp.exp2(m_prev - m_new)
        p = jnp.exp2(s - m_new)
        l_ref[:, c0:c1] = alpha * l_ref[:, c0:c1] + jnp.sum(p, axis=0, keepdims=True)
        pv = jnp.dot(vt_ref[j], p.astype(BF16), preferred_element_type=F32)
        acc_ref[:, c0:c1] = alpha * acc_ref[:, c0:c1] + pv
        m_ref[:, c0:c1] = m_new

    def first_pair(i):
        return first_ref[h * nq + i]

    def query_block(i, qa_ref, qa_next_ref):
        def pair(p, carry):
            j = 2 * p
            scores(j + 1, qa_ref, s1_ref)
            softmax_pv(j, s0_ref)
            scores(j + 2, qa_ref, s0_ref)
            softmax_pv(j + 1, s1_ref)
            return carry

        m_ref[...] = jnp.full_like(m_ref, -jnp.inf)
        l_ref[...] = jnp.zeros_like(l_ref)
        acc_ref[...] = jnp.zeros_like(acc_ref)
        lax.fori_loop(first_pair(i), i, pair, 0)
        scores(2 * i + 1, qa_ref, s1_ref, c0=tkv)
        softmax_pv(2 * i, s0_ref, 0, tkv, key_offset=0)
        softmax_pv(2 * i, s0_ref, tkv, tq)
        nxt = jnp.minimum(i + 1, nq - 1)
        load_queries(nxt, qa_next_ref)
        scores(2 * first_pair(nxt), qa_next_ref, s0_ref)
        softmax_pv(2 * i + 1, s1_ref, tkv, tq, key_offset=tkv)
        rows = pl.ds(pl.multiple_of(i * tq, tq), tq)
        o = (acc_ref[...] / l_ref[...]).T
        o_ref[rows, :] = (o * ga_ref[rows, :]).astype(o_ref.dtype)

    def two_query_blocks(t, carry):
        query_block(2 * t, qa0_ref, qa1_ref)
        query_block(2 * t + 1, qa1_ref, qa0_ref)
        return carry

    assert nq % 2 == 0
    load_queries(0, qa0_ref)
    scores(2 * first_pair(0), qa0_ref, s0_ref)
    lax.fori_loop(0, nq // 2, two_query_blocks, 0)


def _attn(first_pair, q, k, vt, qall, kaug, ga, *, tq, tkv):
    s = q.shape[0]
    assert tq == 2 * tkv and vt.shape == (s // tkv, W_ATTN, tkv)
    head = lambda h, f: (0, h)
    grid_spec = pltpu.PrefetchScalarGridSpec(
        num_scalar_prefetch=1,
        grid=(N_HEADS,),
        in_specs=[pl.BlockSpec((s, HEAD_DIM), head),
                  pl.BlockSpec((s, LANES), lambda h, f: (0, 0)),
                  pl.BlockSpec((s, HEAD_DIM), head),
                  pl.BlockSpec((s, LANES), lambda h, f: (0, 0)),
                  pl.BlockSpec((s // tkv, HEAD_DIM, tkv), lambda h, f: (0, h, 0)),
                  pl.BlockSpec((s, HEAD_DIM), head)],
        out_specs=pl.BlockSpec((s, HEAD_DIM), head),
        scratch_shapes=[pltpu.VMEM((tq, 2 * HEAD_DIM), BF16),
                        pltpu.VMEM((tq, 2 * HEAD_DIM), BF16),
                        pltpu.VMEM((tkv, tq), F32),
                        pltpu.VMEM((tkv, tq), F32),
                        pltpu.VMEM((HEAD_DIM, tq), F32),
                        pltpu.VMEM((1, tq), F32),
                        pltpu.VMEM((1, tq), F32)])
    return pl.pallas_call(
        functools.partial(_attn_kernel, tq=tq, tkv=tkv, nq=s // tq),
        grid_spec=grid_spec,
        out_shape=jax.ShapeDtypeStruct((s, W_ATTN), BF16),
        compiler_params=_cparams(("parallel",), 56),
        name="forgetting_attention",
    )(first_pair, q, qall, k, kaug, vt, ga)


_N_LC = W_CONV // LANES
_TAPS_PAD = 32


def _conv_kernel(ucur_ref, uprev_ref, gc_ref, wdw_ref, bdw_ref, g_ref, b_ref, wo_ref, o_ref, wobf_ref,
                 buf_ref, c_ref, *, tr, rc):
    i = pl.program_id(0)
    wobf_ref[...] = wo_ref[...].astype(BF16)
    lanes = lambda c: slice(c * LANES, (c + 1) * LANES)
    for c in range(_N_LC):
        buf_ref[c, 0:HALO, :] = jnp.where(i == 0, 0.0, uprev_ref[:, lanes(c)])
        buf_ref[c, HALO:HALO + tr, :] = ucur_ref[:, lanes(c)]
    lead = HALO - (CONV_WIDTH - 1)
    win_rows = rc + HALO

    def lane_chunk(c, carry):
        for r0 in range(0, tr, rc):
            win = buf_ref[c, r0:r0 + win_rows, :]
            acc = jnp.broadcast_to(bdw_ref[c], (rc, LANES))
            for b in range(SUBLANES):
                rot = win if b == 0 else pltpu.roll(win, win_rows - b, axis=0)
                for a in range(win_rows // SUBLANES):
                    j = SUBLANES * a + b - lead
                    if 0 <= j < CONV_WIDTH:
                        acc = acc + wdw_ref[c, j:j + 1, :] * rot[SUBLANES * a:SUBLANES * a + rc, :]
            c_ref[c, r0:r0 + rc, :] = acc
        return carry

    lax.fori_loop(0, _N_LC, lane_chunk, 0)

    total = c_ref[0]
    for c in range(1, _N_LC):
        total = total + c_ref[c]
    mu = jnp.sum(total, axis=-1, keepdims=True) * (1.0 / W_CONV)
    sq = jnp.zeros_like(total)
    for c in range(_N_LC):
        d = c_ref[c] - mu
        sq = sq + d * d
    rstd = lax.rsqrt(jnp.sum(sq, axis=-1, keepdims=True) * (1.0 / W_CONV) + LN_EPS)
    for c in range(_N_LC):
        y = (c_ref[c] - mu) * rstd * g_ref[:, lanes(c)] + b_ref[:, lanes(c)]
        y = y * _sigmoid(y)
        o_ref[:, lanes(c)] = (y * gc_ref[:, lanes(c)]).astype(o_ref.dtype)


def _conv(u, gc, w_dw, b_dw, g_cn, b_cn, w_out, *, tr):
    s, w = u.shape
    e, d = w_out.shape
    slab = e // (s // tr)
    assert slab * (s // tr) == e and slab % (2 * SUBLANES) == 0
    wdw = jnp.pad(w_dw, ((0, _TAPS_PAD - CONV_WIDTH), (0, 0))).reshape(_TAPS_PAD, _N_LC, LANES)
    wdw = wdw.transpose(1, 0, 2)
    bdw = b_dw.reshape(_N_LC, 1, LANES)
    vec = lambda: pl.BlockSpec((1, w), lambda i: (0, 0))
    return pl.pallas_call(
        functools.partial(_conv_kernel, tr=tr, rc=128),
        grid=(s // tr,),
        in_specs=[pl.BlockSpec((tr, w), lambda i: (i, 0)),
                  pl.BlockSpec((HALO, w), lambda i: (jnp.maximum(i * (tr // HALO) - 1, 0), 0)),
                  pl.BlockSpec((tr, w), lambda i: (i, 0)),
                  pl.BlockSpec((_N_LC, _TAPS_PAD, LANES), lambda i: (0, 0, 0)),
                  pl.BlockSpec((_N_LC, 1, LANES), lambda i: (0, 0, 0)),
                  vec(), vec(),
                  pl.BlockSpec((slab, d), lambda i: (i, 0))],
        out_specs=[pl.BlockSpec((tr, w), lambda i: (i, 0)),
                   pl.BlockSpec((slab, d), lambda i: (i, 0))],
        out_shape=[jax.ShapeDtypeStruct((s, w), BF16),
                   jax.ShapeDtypeStruct((e, d), BF16)],
        scratch_shapes=[pltpu.VMEM((_N_LC, HALO + tr, LANES), F32), pltpu.VMEM((_N_LC, tr, LANES), F32)],
        compiler_params=_cparams(("parallel",), 40),
        name="conformer_conv",
    )(u, u, gc, wdw, bdw, g_cn, b_cn, w_out)


def _out_kernel(yc_ref, ya_ref, w_ref, x_ref, bo_ref, g_ref, b_ref, o_ref, *, n_split):
    rows_per = o_ref.shape[0] // n_split
    for r in range(n_split):
        rows = slice(r * rows_per, (r + 1) * rows_per)
        sub = jnp.dot(yc_ref[rows, :], w_ref[0:W_CONV, :], preferred_element_type=F32)
        sub = sub + jnp.dot(ya_ref[rows, :], w_ref[W_CONV:W_CONV + W_ATTN, :], preferred_element_type=F32)
        h = DN_ALPHA * x_ref[rows, :] + (sub + bo_ref[...])
        o_ref[rows, :] = _layer_norm(h, g_ref[...], b_ref[...])


def _out(yc, ya, w_out_bf, x2, b_out, g_post, b_post, *, tm):
    s, d = x2.shape
    e = w_out_bf.shape[0]
    vec = lambda: pl.BlockSpec((1, d), lambda i: (0, 0))
    return pl.pallas_call(
        functools.partial(_out_kernel, n_split=2),
        grid=(s // tm,),
        in_specs=[pl.BlockSpec((tm, W_CONV), lambda i: (i, 0)),
                  pl.BlockSpec((tm, W_ATTN), lambda i: (i, 0)),
                  pl.BlockSpec((e, d), lambda i: (0, 0), pipeline_mode=pl.Buffered(1)),
                  pl.BlockSpec((tm, d), lambda i: (i, 0)),
                  vec(), vec(), vec()],
        out_specs=pl.BlockSpec((tm, d), lambda i: (i, 0)),
        out_shape=jax.ShapeDtypeStruct((s, d), F32),
        compiler_params=_cparams(("parallel",), 56),
        name="out_proj_deepnorm",
    )(yc, ya, w_out_bf, x2, b_out, g_post, b_post)


def kernel(x, w_in, b_in, w_dw, b_dw, g_conv_norm, b_conv_norm, w_out, b_out, g_post, b_post):
    assert x.shape == (1, SEQ, D_MODEL)
    x2 = x[0]
    row = lambda v: v.reshape(1, -1)
    bias = lambda a, n: row(b_in[a:a + n])
    lane_pad = lambda a: jnp.pad(a, ((0, 0), (0, LANES - a.shape[1])))

    o_val, o_gate, o_gc, o_q, o_k, o_v, o_f = 0, 2048, 4096, 6144, 8192, 10240, 12288
    o_ga = o_f + N_HEADS
    wt = w_in.T
    wt_f = jnp.pad(wt[o_f:o_ga], ((0, LANES - N_HEADS), (0, 0))).astype(BF16)
    b_f = lane_pad(bias(o_f, N_HEADS))

    tq, tkv = 1024, 512
    tm, tn = 2048, 512
    q_scale = LOG2E / math.sqrt(HEAD_DIM)
    silu = lambda z: z * _sigmoid(z)
    x_bf, kaug, qall, cfirst, clast = _cum(x2, wt_f, b_f, tb=tkv)
    u = _proj(x_bf, wt, [o_val, o_gate], [bias(o_val, W_CONV), bias(o_gate, W_CONV)], W_CONV,
              lambda a, g: a * _sigmoid(g), F32, tm=tm // 2, tn=tn, name="proj_glu")
    gc = _proj(x_bf, wt, [o_gc], [bias(o_gc, W_CONV)], W_CONV, silu, F32,
               tm=tm, tn=tn, name="proj_gate_conv")
    q, qn2 = _proj(x_bf, wt, [o_q], [bias(o_q, W_ATTN)], W_ATTN, lambda z: z * q_scale, BF16,
                   tm=tm, tn=tn, name="proj_q", head_norms=True)
    k, kn2 = _proj(x_bf, wt, [o_k], [bias(o_k, W_ATTN)], W_ATTN, lambda z: z, BF16,
                   tm=tm, tn=tn, name="proj_k", head_norms=True)
    vt = _proj_t(x_bf, wt, o_v, b_in[o_v:o_v + W_ATTN].reshape(-1, 1), W_ATTN,
                 ts=tkv, tn=1024, blocks_per_step=2, name="proj_v_transposed")
    ga = _proj(x_bf, wt, [o_ga], [bias(o_ga, W_ATTN)], W_ATTN, silu, F32,
               tm=tm, tn=tn, name="proj_gate_attn")

    first_pair = _plan(lane_pad(qn2), lane_pad(kn2), cfirst[:, 0], clast[:, 0],
                       nq=SEQ // tq, q_per_norm_block=tm // tq, kv_per_q=tq // tkv)
    first_pair = first_pair[:, :N_HEADS].T.reshape(-1)

    y_conv, w_out_bf = _conv(u, gc, w_dw, row(b_dw), row(g_conv_norm), row(b_conv_norm), w_out, tr=256)
    y_attn = _attn(first_pair, q, k, vt, qall, kaug, ga, tq=tq, tkv=tkv)
    out = _out(y_conv, y_attn, w_out_bf, x2, row(b_out), row(g_post), row(b_post), tm=512)
    return out[None]
```

```python
import functools
import math

import jax
import jax.numpy as jnp
from jax import lax
from jax.experimental import pallas as pl
from jax.experimental.pallas import tpu as pltpu

D_MODEL = 2048
SEQ = 8192
W_CONV = 2048
W_ATTN = 2048
HEAD_DIM = 128
N_HEADS = W_ATTN // HEAD_DIM
CONV_WIDTH = 31
LN_EPS = 1e-5
DN_ALPHA = 2.0 ** 0.25
LOG2E = math.log2(math.e)
LANES = 128
SUBLANES = 8
HALO = 32
MIB = 1024 * 1024

F32_EXP2_UNDERFLOW = 152.0
SKIP_SLACK = 2.0
NORM_INFLATE = 1.02

F32 = jnp.float32
BF16 = jnp.bfloat16

_NT = (((1,), (1,)), ((), ()))


def _cparams(semantics, vmem_mib):
    return pltpu.CompilerParams(dimension_semantics=semantics,
                                vmem_limit_bytes=vmem_mib * MIB)


def _sigmoid(x):
    return 1.0 / (1.0 + jnp.exp(-x))


def _layer_norm(h, g, b):
    mu = jnp.mean(h, axis=-1, keepdims=True)
    d = h - mu
    var = jnp.mean(d * d, axis=-1, keepdims=True)
    return d * lax.rsqrt(var + LN_EPS) * g + b


def _proj_kernel(x_ref, *refs, n_w, epilogue, head_norms, n_split):
    w_refs, b_refs, o_ref = refs[:n_w], refs[n_w:2 * n_w], refs[2 * n_w]
    ws = [w[...].astype(BF16) for w in w_refs]
    rows_per = o_ref.shape[0] // n_split
    for r in range(n_split):
        rows = slice(r * rows_per, (r + 1) * rows_per)
        zs = [lax.dot_general(x_ref[rows, :], w, _NT, preferred_element_type=F32) + b[...]
              for w, b in zip(ws, b_refs)]
        o_ref[rows, :] = epilogue(*zs).astype(o_ref.dtype)
    if head_norms:
        y = o_ref[...]
        nrm_ref = refs[2 * n_w + 1]
        sq = y.astype(F32)
        sq = sq * sq
        lane = lax.broadcasted_iota(jnp.int32, (1, LANES), 1)
        out = jnp.zeros((1, LANES), F32)
        for hh in range(y.shape[1] // HEAD_DIM):
            n2 = jnp.sum(sq[:, hh * HEAD_DIM:(hh + 1) * HEAD_DIM], axis=-1, keepdims=True)
            out = jnp.where(lane == hh, jnp.max(n2, axis=0, keepdims=True), out)
        nrm_ref[0] = out


def _proj(x_bf, wt, row_offsets, bs, n, epilogue, out_dtype, *, tm, tn, name, head_norms=False):
    m, k = x_bf.shape
    n_w = len(row_offsets)
    assert all(off % SUBLANES == 0 for off in row_offsets)

    def weight_spec(off):
        if off % tn == 0:
            return pl.BlockSpec((tn, k), lambda i, j: (off // tn + j, 0))
        return pl.BlockSpec((pl.Element(tn), pl.Element(k)),
                            lambda i, j: (pl.multiple_of(off + j * tn, SUBLANES), 0))

    in_specs = [pl.BlockSpec((tm, k), lambda i, j: (i, 0))]
    in_specs += [weight_spec(off) for off in row_offsets]
    in_specs += [pl.BlockSpec((1, tn), lambda i, j: (0, j)) for _ in bs]
    out_specs = [pl.BlockSpec((tm, tn), lambda i, j: (i, j))]
    out_shape = [jax.ShapeDtypeStruct((m, n), out_dtype)]
    nj = n // tn
    if head_norms:
        out_specs.append(pl.BlockSpec((1, 1, LANES), lambda i, j: (i * nj + j, 0, 0)))
        out_shape.append(jax.ShapeDtypeStruct((m // tm * nj, 1, LANES), F32))
    res = pl.pallas_call(
        functools.partial(_proj_kernel, n_w=n_w, epilogue=epilogue, head_norms=head_norms, n_split=tm // 256),
        grid=(m // tm, nj), in_specs=in_specs, out_specs=out_specs, out_shape=out_shape,
        compiler_params=_cparams(("parallel", "arbitrary"), 56),
        name=name,
    )(x_bf, *([wt] * n_w), *bs)
    if not head_norms:
        return res[0]
    heads_per_tile = tn // HEAD_DIM
    norms = res[1].reshape(m // tm, nj, LANES)[:, :, :heads_per_tile].reshape(m // tm, n // HEAD_DIM)
    return res[0], norms


def _proj_t_kernel(w_ref, x_ref, b_ref, o_ref, wbf_ref):
    @pl.when(pl.program_id(1) == 0)
    def _():
        wbf_ref[...] = w_ref[...].astype(BF16)

    ts = o_ref.shape[2]
    for r in range(o_ref.shape[0]):
        z = lax.dot_general(wbf_ref[...], x_ref[r * ts:(r + 1) * ts, :], _NT,
                            preferred_element_type=F32)
        o_ref[r] = (z + b_ref[...]).astype(o_ref.dtype)


def _proj_t(x_bf, wt, row_offset, b_col, n, *, ts, tn, blocks_per_step, name):
    m, k = x_bf.shape
    assert row_offset % tn == 0
    bps = blocks_per_step
    return pl.pallas_call(
        _proj_t_kernel,
        grid=(n // tn, m // (ts * bps)),
        in_specs=[pl.BlockSpec((tn, k), lambda a, i: (row_offset // tn + a, 0)),
                  pl.BlockSpec((ts * bps, k), lambda a, i: (i, 0)),
                  pl.BlockSpec((tn, 1), lambda a, i: (a, 0))],
        out_specs=pl.BlockSpec((bps, tn, ts), lambda a, i: (i, a, 0)),
        out_shape=jax.ShapeDtypeStruct((m // ts, n, ts), BF16),
        scratch_shapes=[pltpu.VMEM((tn, k), BF16)],
        compiler_params=_cparams(("parallel", "arbitrary"), 56),
        name=name,
    )(wt, x_bf, b_col)


_FIELD = N_HEADS


def _split3(c):
    hi = c.astype(BF16).astype(F32)
    r1 = c - hi
    mid = r1.astype(BF16).astype(F32)
    lo = (r1 - mid).astype(BF16).astype(F32)
    return hi, mid, lo


def _cum_kernel(x_ref, wf_ref, bf_ref, xbf_ref, kaug_ref, qall_ref, cfirst_ref, clast_ref, carry_ref, *, tb):
    @pl.when(pl.program_id(0) == 0)
    def _():
        carry_ref[...] = jnp.zeros_like(carry_ref)

    x_bf = x_ref[...].astype(BF16)
    xbf_ref[...] = x_bf
    f = lax.dot_general(x_bf, wf_ref[...], _NT, preferred_element_type=F32) + bf_ref[...]
    log_f = jnp.minimum(f, 0.0) - jnp.log1p(jnp.exp(-jnp.abs(f)))
    row = lax.broadcasted_iota(jnp.int32, (tb, tb), 0)
    col = lax.broadcasted_iota(jnp.int32, (tb, tb), 1)
    lower = (col <= row).astype(BF16)
    parts = jnp.concatenate(_split3(log_f), axis=1).astype(BF16)
    sums = jnp.dot(lower, parts, preferred_element_type=F32)
    cum = (sums[:, :LANES] + sums[:, LANES:2 * LANES]) + sums[:, 2 * LANES:] + carry_ref[...]
    carry_ref[...] = cum[tb - 1:tb, :]

    cum2 = cum * LOG2E
    cfirst_ref[0] = cum2[0:1, :]
    clast_ref[0] = cum2[tb - 1:tb, :]
    hi, mid, lo = _split3(cum2)
    lane = lax.broadcasted_iota(jnp.int32, (tb, LANES), 1)
    field = lambda v, n: pltpu.roll(v, n * _FIELD, axis=1)
    k_cols = jnp.where(lane < _FIELD, -hi,
             jnp.where(lane < 2 * _FIELD, -field(mid, 1),
             jnp.where(lane < 3 * _FIELD, -field(lo, 2),
             jnp.where(lane < 6 * _FIELD, 1.0, 0.0))))
    q_cols = jnp.where(lane < 3 * _FIELD, 1.0,
             jnp.where(lane < 4 * _FIELD, field(hi, 3),
             jnp.where(lane < 5 * _FIELD, field(mid, 4),
             jnp.where(lane < 6 * _FIELD, field(lo, 5), 0.0))))
    kaug_ref[...] = k_cols.astype(BF16)
    qall_ref[...] = q_cols.astype(BF16)


def _cum(x2, wf, bf, *, tb):
    m, k = x2.shape
    aug_shape = jax.ShapeDtypeStruct((m, LANES), BF16)
    aug_spec = pl.BlockSpec((tb, LANES), lambda i: (i, 0))
    edge_shape = jax.ShapeDtypeStruct((m // tb, 1, LANES), F32)
    edge_spec = pl.BlockSpec((1, 1, LANES), lambda i: (i, 0, 0))
    return pl.pallas_call(
        functools.partial(_cum_kernel, tb=tb),
        grid=(m // tb,),
        in_specs=[pl.BlockSpec((tb, k), lambda i: (i, 0)),
                  pl.BlockSpec((LANES, k), lambda i: (0, 0)),
                  pl.BlockSpec((1, LANES), lambda i: (0, 0))],
        out_specs=[pl.BlockSpec((tb, k), lambda i: (i, 0)), aug_spec, aug_spec, edge_spec, edge_spec],
        out_shape=[jax.ShapeDtypeStruct((m, k), BF16), aug_shape, aug_shape, edge_shape, edge_shape],
        scratch_shapes=[pltpu.VMEM((1, LANES), F32)],
        compiler_params=_cparams(("arbitrary",), 32),
        name="forget_cumsum",
    )(x2, wf, bf)


def _plan_kernel(qn2_ref, kn2_ref, cfirst_ref, clast_ref, o_ref, *, nq, q_per_norm_block, kv_per_q):
    kn = jnp.sqrt(jnp.max(kn2_ref[...], axis=0, keepdims=True))
    clast = clast_ref[...]
    n_kv = clast.shape[0]
    blk = lax.broadcasted_iota(jnp.int32, (n_kv, LANES), 0)
    assert kv_per_q == 2
    pair_shift = jnp.full((1, LANES), 1, jnp.int32)
    for i in range(nq):
        qn = jnp.sqrt(qn2_ref[i // q_per_norm_block:i // q_per_norm_block + 1, :])
        bound = (2.0 * NORM_INFLATE) * qn * kn + (cfirst_ref[kv_per_q * i:kv_per_q * i + 1, :] - clast)
        skip = (bound < -(F32_EXP2_UNDERFLOW + SKIP_SLACK)) & (blk < kv_per_q * i)
        first = jnp.min(jnp.where(skip, n_kv, blk), axis=0, keepdims=True)
        o_ref[i:i + 1, :] = lax.shift_right_logical(first, pair_shift)


def _plan(qn2, kn2, cfirst, clast, *, nq, q_per_norm_block, kv_per_q):
    full = lambda a: pl.BlockSpec(a.shape, lambda: (0,) * a.ndim)
    args = (qn2, kn2, cfirst, clast)
    return pl.pallas_call(
        functools.partial(_plan_kernel, nq=nq, q_per_norm_block=q_per_norm_block, kv_per_q=kv_per_q),
        in_specs=[full(a) for a in args],
        out_specs=pl.BlockSpec((nq, LANES), lambda: (0, 0)),
        out_shape=jax.ShapeDtypeStruct((nq, LANES), jnp.int32),
        name="attention_plan",
    )(*args)


def _attn_kernel(first_ref, q_ref, qall_ref, k_ref, kaug_ref, vt_ref, ga_ref, o_ref,
                 qa0_ref, qa1_ref, s0_ref, s1_ref, acc_ref, m_ref, l_ref, *, tq, tkv, nq):
    h = pl.program_id(0)
    lane = lax.broadcasted_iota(jnp.int32, (tq, LANES), 1)
    own = (lane & (_FIELD - 1)) == h

    def load_queries(i, qa_ref):
        r0 = pl.multiple_of(i * tq, tq)
        qall = qall_ref[pl.ds(r0, tq), :]
        qa_ref[...] = jnp.concatenate(
            [q_ref[pl.ds(r0, tq), :], jnp.where(own, qall, jnp.zeros_like(qall))], axis=1)

    def scores(j, qa_ref, s_ref, c0=0):
        r0 = pl.multiple_of(j * tkv, tkv)
        ka = jnp.concatenate([k_ref[pl.ds(r0, tkv), :], kaug_ref[pl.ds(r0, tkv), :]], axis=1)
        s_ref[:, c0:] = lax.dot_general(ka, qa_ref[c0:, :], _NT, preferred_element_type=F32)

    def softmax_pv(j, s_ref, c0=0, c1=tq, key_offset=None):
        s = s_ref[:, c0:c1]
        if key_offset is not None:
            key_pos = lax.broadcasted_iota(jnp.int32, s.shape, 0) + key_offset
            qry_pos = lax.broadcasted_iota(jnp.int32, s.shape, 1) + c0
            s = jnp.where(key_pos <= qry_pos, s, -jnp.inf)
        m_prev = m_ref[:, c0:c1]
        m_new = jnp.maximum(m_prev, jnp.max(s, axis=0, keepdims=True))
        alpha = jnp.exp2(m_prev - m_new)
        p = jnp.exp2(s - m_new)
        l_ref[:, c0:c1] = alpha * l_ref[:, c0:c1] + jnp.sum(p, axis=0, keepdims=True)
        pv = jnp.dot(vt_ref[j], p.astype(BF16), preferred_element_type=F32)
        acc_ref[:, c0:c1] = alpha * acc_ref[:, c0:c1] + pv
        m_ref[:, c0:c1] = m_new

    def first_pair(i):
        return first_ref[h * nq + i]

    def query_block(i, qa_ref, qa_next_ref):
        def pair(p, carry):
            j = 2 * p
            scores(j + 1, qa_ref, s1_ref)
            softmax_pv(j, s0_ref)
            scores(j + 2, qa_ref, s0_ref)
            softmax_pv(j + 1, s1_ref)
            return carry

        m_ref[...] = jnp.full_like(m_ref, -jnp.inf)
        l_ref[...] = jnp.zeros_like(l_ref)
        acc_ref[...] = jnp.zeros_like(acc_ref)
        lax.fori_loop(first_pair(i), i, pair, 0)
        scores(2 * i + 1, qa_ref, s1_ref, c0=tkv)
        softmax_pv(2 * i, s0_ref, 0, tkv, key_offset=0)
        softmax_pv(2 * i, s0_ref, tkv, tq)
        nxt = jnp.minimum(i + 1, nq - 1)
        load_queries(nxt, qa_next_ref)
        scores(2 * first_pair(nxt), qa_next_ref, s0_ref)
        softmax_pv(2 * i + 1, s1_ref, tkv, tq, key_offset=tkv)
        rows = pl.ds(pl.multiple_of(i * tq, tq), tq)
        o = (acc_ref[...] / l_ref[...]).T
        o_ref[rows, :] = (o * ga_ref[rows, :]).astype(o_ref.dtype)

    def two_query_blocks(t, carry):
        query_block(2 * t, qa0_ref, qa1_ref)
        query_block(2 * t + 1, qa1_ref, qa0_ref)
        return carry

    assert nq % 2 == 0
    load_queries(0, qa0_ref)
    scores(2 * first_pair(0), qa0_ref, s0_ref)
    lax.fori_loop(0, nq // 2, two_query_blocks, 0)


def _attn(first_pair, q, k, vt, qall, kaug, ga, *, tq, tkv):
    s = q.shape[0]
    assert tq == 2 * tkv and vt.shape == (s // tkv, W_ATTN, tkv)
    head = lambda h, f: (0, h)
    grid_spec = pltpu.PrefetchScalarGridSpec(
        num_scalar_prefetch=1,
        grid=(N_HEADS,),
        in_specs=[pl.BlockSpec((s, HEAD_DIM), head),
                  pl.BlockSpec((s, LANES), lambda h, f: (0, 0)),
                  pl.BlockSpec((s, HEAD_DIM), head),
                  pl.BlockSpec((s, LANES), lambda h, f: (0, 0)),
                  pl.BlockSpec((s // tkv, HEAD_DIM, tkv), lambda h, f: (0, h, 0)),
                  pl.BlockSpec((s, HEAD_DIM), head)],
        out_specs=pl.BlockSpec((s, HEAD_DIM), head),
        scratch_shapes=[pltpu.VMEM((tq, 2 * HEAD_DIM), BF16),
                        pltpu.VMEM((tq, 2 * HEAD_DIM), BF16),
                        pltpu.VMEM((tkv, tq), F32),
                        pltpu.VMEM((tkv, tq), F32),
                        pltpu.VMEM((HEAD_DIM, tq), F32),
                        pltpu.VMEM((1, tq), F32),
                        pltpu.VMEM((1, tq), F32)])
    return pl.pallas_call(
        functools.partial(_attn_kernel, tq=tq, tkv=tkv, nq=s // tq),
        grid_spec=grid_spec,
        out_shape=jax.ShapeDtypeStruct((s, W_ATTN), BF16),
        compiler_params=_cparams(("parallel",), 56),
        name="forgetting_attention",
    )(first_pair, q, qall, k, kaug, vt, ga)


_N_LC = W_CONV // LANES
_TAPS_PAD = 32


def _conv_kernel(ucur_ref, uprev_ref, gc_ref, wdw_ref, bdw_ref, g_ref, b_ref, wo_ref, o_ref, wobf_ref,
                 buf_ref, c_ref, *, tr, rc):
    i = pl.program_id(0)
    wobf_ref[...] = wo_ref[...].astype(BF16)
    lanes = lambda c: slice(c * LANES, (c + 1) * LANES)
    for c in range(_N_LC):
        buf_ref[c, 0:HALO, :] = jnp.where(i == 0, 0.0, uprev_ref[:, lanes(c)])
        buf_ref[c, HALO:HALO + tr, :] = ucur_ref[:, lanes(c)]
    lead = HALO - (CONV_WIDTH - 1)
    win_rows = rc + HALO

    def lane_chunk(c, carry):
        for r0 in range(0, tr, rc):
            win = buf_ref[c, r0:r0 + win_rows, :]
            acc = jnp.broadcast_to(bdw_ref[c], (rc, LANES))
            for b in range(SUBLANES):
                rot = win if b == 0 else pltpu.roll(win, win_rows - b, axis=0)
                for a in range(win_rows // SUBLANES):
                    j = SUBLANES * a + b - lead
                    if 0 <= j < CONV_WIDTH:
                        acc = acc + wdw_ref[c, j:j + 1, :] * rot[SUBLANES * a:SUBLANES * a + rc, :]
            c_ref[c, r0:r0 + rc, :] = acc
        return carry

    lax.fori_loop(0, _N_LC, lane_chunk, 0)

    total = c_ref[0]
    for c in range(1, _N_LC):
        total = total + c_ref[c]
    mu = jnp.sum(total, axis=-1, keepdims=True) * (1.0 / W_CONV)
    sq = jnp.zeros_like(total)
    for c in range(_N_LC):
        d = c_ref[c] - mu
        sq = sq + d * d
    rstd = lax.rsqrt(jnp.sum(sq, axis=-1, keepdims=True) * (1.0 / W_CONV) + LN_EPS)
    for c in range(_N_LC):
        y = (c_ref[c] - mu) * rstd * g_ref[:, lanes(c)] + b_ref[:, lanes(c)]
        y = y * _sigmoid(y)
        o_ref[:, lanes(c)] = (y * gc_ref[:, lanes(c)]).astype(o_ref.dtype)


def _conv(u, gc, w_dw, b_dw, g_cn, b_cn, w_out, *, tr):
    s, w = u.shape
    e, d = w_out.shape
    slab = e // (s // tr)
    assert slab * (s // tr) == e and slab % (2 * SUBLANES) == 0
    wdw = jnp.pad(w_dw, ((0, _TAPS_PAD - CONV_WIDTH), (0, 0))).reshape(_TAPS_PAD, _N_LC, LANES)
    wdw = wdw.transpose(1, 0, 2)
    bdw = b_dw.reshape(_N_LC, 1, LANES)
    vec = lambda: pl.BlockSpec((1, w), lambda i: (0, 0))
    return pl.pallas_call(
        functools.partial(_conv_kernel, tr=tr, rc=128),
        grid=(s // tr,),
        in_specs=[pl.BlockSpec((tr, w), lambda i: (i, 0)),
                  pl.BlockSpec((HALO, w), lambda i: (jnp.maximum(i * (tr // HALO) - 1, 0), 0)),
                  pl.BlockSpec((tr, w), lambda i: (i, 0)),
                  pl.BlockSpec((_N_LC, _TAPS_PAD, LANES), lambda i: (0, 0, 0)),
                  pl.BlockSpec((_N_LC, 1, LANES), lambda i: (0, 0, 0)),
                  vec(), vec(),
                  pl.BlockSpec((slab, d), lambda i: (i, 0))],
        out_specs=[pl.BlockSpec((tr, w), lambda i: (i, 0)),
                   pl.BlockSpec((slab, d), lambda i: (i, 0))],
        out_shape=[jax.ShapeDtypeStruct((s, w), BF16),
                   jax.ShapeDtypeStruct((e, d), BF16)],
        scratch_shapes=[pltpu.VMEM((_N_LC, HALO + tr, LANES), F32), pltpu.VMEM((_N_LC, tr, LANES), F32)],
        compiler_params=_cparams(("parallel",), 40),
        name="conformer_conv",
    )(u, u, gc, wdw, bdw, g_cn, b_cn, w_out)


def _out_kernel(yc_ref, ya_ref, w_ref, x_ref, bo_ref, g_ref, b_ref, o_ref, *, n_split):
    rows_per = o_ref.shape[0] // n_split
    for r in range(n_split):
        rows = slice(r * rows_per, (r + 1) * rows_per)
        sub = jnp.dot(yc_ref[rows, :], w_ref[0:W_CONV, :], preferred_element_type=F32)
        sub = sub + jnp.dot(ya_ref[rows, :], w_ref[W_CONV:W_CONV + W_ATTN, :], preferred_element_type=F32)
        h = DN_ALPHA * x_ref[rows, :] + (sub + bo_ref[...])
        o_ref[rows, :] = _layer_norm(h, g_ref[...], b_ref[...])


def _out(yc, ya, w_out_bf, x2, b_out, g_post, b_post, *, tm):
    s, d = x2.shape
    e = w_out_bf.shape[0]
    vec = lambda: pl.BlockSpec((1, d), lambda i: (0, 0))
    return pl.pallas_call(
        functools.partial(_out_kernel, n_split=2),
        grid=(s // tm,),
        in_specs=[pl.BlockSpec((tm, W_CONV), lambda i: (i, 0)),
                  pl.BlockSpec((tm, W_ATTN), lambda i: (i, 0)),
                  pl.BlockSpec((e, d), lambda i: (0, 0), pipeline_mode=pl.Buffered(1)),
                  pl.BlockSpec((tm, d), lambda i: (i, 0)),
                  vec(), vec(), vec()],
        out_specs=pl.BlockSpec((tm, d), lambda i: (i, 0)),
        out_shape=jax.ShapeDtypeStruct((s, d), F32),
        compiler_params=_cparams(("parallel",), 56),
        name="out_proj_deepnorm",
    )(yc, ya, w_out_bf, x2, b_out, g_post, b_post)


def kernel(x, w_in, b_in, w_dw, b_dw, g_conv_norm, b_conv_norm, w_out, b_out, g_post, b_post):
    assert x.shape == (1, SEQ, D_MODEL)
    x2 = x[0]
    row = lambda v: v.reshape(1, -1)
    bias = lambda a, n: row(b_in[a:a + n])
    lane_pad = lambda a: jnp.pad(a, ((0, 0), (0, LANES - a.shape[1])))

    o_val, o_gate, o_gc, o_q, o_k, o_v, o_f = 0, 2048, 4096, 6144, 8192, 10240, 12288
    o_ga = o_f + N_HEADS
    wt = w_in.T
    wt_f = jnp.pad(wt[o_f:o_ga], ((0, LANES - N_HEADS), (0, 0))).astype(BF16)
    b_f = lane_pad(bias(o_f, N_HEADS))

    tq, tkv = 1024, 512
    tm, tn = 2048, 512
    q_scale = LOG2E / math.sqrt(HEAD_DIM)
    silu = lambda z: z * _sigmoid(z)
    x_bf, kaug, qall, cfirst, clast = _cum(x2, wt_f, b_f, tb=tkv)
    u = _proj(x_bf, wt, [o_val, o_gate], [bias(o_val, W_CONV), bias(o_gate, W_CONV)], W_CONV,
              lambda a, g: a * _sigmoid(g), F32, tm=tm, tn=tn, name="proj_glu")
    gc = _proj(x_bf, wt, [o_gc], [bias(o_gc, W_CONV)], W_CONV, silu, F32,
               tm=tm, tn=tn, name="proj_gate_conv")
    q, qn2 = _proj(x_bf, wt, [o_q], [bias(o_q, W_ATTN)], W_ATTN, lambda z: z * q_scale, BF16,
                   tm=tm, tn=tn, name="proj_q", head_norms=True)
    k, kn2 = _proj(x_bf, wt, [o_k], [bias(o_k, W_ATTN)], W_ATTN, lambda z: z, BF16,
                   tm=tm, tn=tn, name="proj_k", head_norms=True)
    vt = _proj_t(x_bf, wt, o_v, b_in[o_v:o_v + W_ATTN].reshape(-1, 1), W_ATTN,
                 ts=tkv, tn=1024, blocks_per_step=4, name="proj_v_transposed")
    ga = _proj(x_bf, wt, [o_ga], [bias(o_ga, W_ATTN)], W_ATTN, silu, F32,
               tm=tm, tn=tn, name="proj_gate_attn")

    first_pair = _plan(lane_pad(qn2), lane_pad(kn2), cfirst[:, 0], clast[:, 0],
                       nq=SEQ // tq, q_per_norm_block=tm // tq, kv_per_q=tq // tkv)
    first_pair = first_pair[:, :N_HEADS].T.reshape(-1)

    y_conv, w_out_bf = _conv(u, gc, w_dw, row(b_dw), row(g_conv_norm), row(b_conv_norm), w_out, tr=256)
    y_attn = _attn(first_pair, q, k, vt, qall, kaug, ga, tq=tq, tkv=tkv)
    out = _out(y_conv, y_attn, w_out_bf, x2, row(b_out), row(g_post), row(b_post), tm=512)
    return out[None]
```

```python
import functools
import math

import jax
import jax.numpy as jnp
from jax import lax
from jax.experimental import pallas as pl
from jax.experimental.pallas import tpu as pltpu

D_MODEL = 2048
SEQ = 8192
W_CONV = 2048
W_ATTN = 2048
HEAD_DIM = 128
N_HEADS = W_ATTN // HEAD_DIM
CONV_WIDTH = 31
LN_EPS = 1e-5
DN_ALPHA = 2.0 ** 0.25
LOG2E = math.log2(math.e)
LANES = 128
SUBLANES = 8
HALO = 32
MIB = 1024 * 1024

F32_EXP2_UNDERFLOW = 152.0
SKIP_SLACK = 2.0
NORM_INFLATE = 1.02

F32 = jnp.float32
BF16 = jnp.bfloat16

_NT = (((1,), (1,)), ((), ()))


def _cparams(semantics, vmem_mib):
    return pltpu.CompilerParams(dimension_semantics=semantics,
                                vmem_limit_bytes=vmem_mib * MIB)


def _sigmoid(x):
    return 1.0 / (1.0 + jnp.exp(-x))


def _layer_norm(h, g, b):
    mu = jnp.mean(h, axis=-1, keepdims=True)
    d = h - mu
    var = jnp.mean(d * d, axis=-1, keepdims=True)
    return d * lax.rsqrt(var + LN_EPS) * g + b


def _proj_kernel(x_ref, *refs, n_w, epilogue, head_norms, n_split):
    w_refs, b_refs, o_ref = refs[:n_w], refs[n_w:2 * n_w], refs[2 * n_w]
    ws = [w[...].astype(BF16) for w in w_refs]
    rows_per = o_ref.shape[0] // n_split
    for r in range(n_split):
        rows = slice(r * rows_per, (r + 1) * rows_per)
        zs = [lax.dot_general(x_ref[rows, :], w, _NT, preferred_element_type=F32) + b[...]
              for w, b in zip(ws, b_refs)]
        o_ref[rows, :] = epilogue(*zs).astype(o_ref.dtype)
    if head_norms:
        y = o_ref[...]
        nrm_ref = refs[2 * n_w + 1]
        sq = y.astype(F32)
        sq = sq * sq
        lane = lax.broadcasted_iota(jnp.int32, (1, LANES), 1)
        out = jnp.zeros((1, LANES), F32)
        for hh in range(y.shape[1] // HEAD_DIM):
            n2 = jnp.sum(sq[:, hh * HEAD_DIM:(hh + 1) * HEAD_DIM], axis=-1, keepdims=True)
            out = jnp.where(lane == hh, jnp.max(n2, axis=0, keepdims=True), out)
        nrm_ref[0] = out


def _proj(x_bf, wt, row_offsets, bs, n, epilogue, out_dtype, *, tm, tn, name, head_norms=False):
    m, k = x_bf.shape
    n_w = len(row_offsets)
    assert all(off % SUBLANES == 0 for off in row_offsets)

    def weight_spec(off):
        if off % tn == 0:
            return pl.BlockSpec((tn, k), lambda i, j: (off // tn + j, 0))
        return pl.BlockSpec((pl.Element(tn), pl.Element(k)),
                            lambda i, j: (pl.multiple_of(off + j * tn, SUBLANES), 0))

    in_specs = [pl.BlockSpec((tm, k), lambda i, j: (i, 0))]
    in_specs += [weight_spec(off) for off in row_offsets]
    in_specs += [pl.BlockSpec((1, tn), lambda i, j: (0, j)) for _ in bs]
    out_specs = [pl.BlockSpec((tm, tn), lambda i, j: (i, j))]
    out_shape = [jax.ShapeDtypeStruct((m, n), out_dtype)]
    nj = n // tn
    if head_norms:
        out_specs.append(pl.BlockSpec((1, 1, LANES), lambda i, j: (i * nj + j, 0, 0)))
        out_shape.append(jax.ShapeDtypeStruct((m // tm * nj, 1, LANES), F32))
    res = pl.pallas_call(
        functools.partial(_proj_kernel, n_w=n_w, epilogue=epilogue, head_norms=head_norms, n_split=tm // 256),
        grid=(m // tm, nj), in_specs=in_specs, out_specs=out_specs, out_shape=out_shape,
        compiler_params=_cparams(("parallel", "arbitrary"), 56),
        name=name,
    )(x_bf, *([wt] * n_w), *bs)
    if not head_norms:
        return res[0]
    heads_per_tile = tn // HEAD_DIM
    norms = res[1].reshape(m // tm, nj, LANES)[:, :, :heads_per_tile].reshape(m // tm, n // HEAD_DIM)
    return res[0], norms


def _proj_t_kernel(w_ref, x_ref, b_ref, o_ref, wbf_ref):
    @pl.when(pl.program_id(1) == 0)
    def _():
        wbf_ref[...] = w_ref[...].astype(BF16)

    ts = o_ref.shape[2]
    for r in range(o_ref.shape[0]):
        z = lax.dot_general(wbf_ref[...], x_ref[r * ts:(r + 1) * ts, :], _NT,
                            preferred_element_type=F32)
        o_ref[r] = (z + b_ref[...]).astype(o_ref.dtype)


def _proj_t(x_bf, wt, row_offset, b_col, n, *, ts, tn, blocks_per_step, name):
    m, k = x_bf.shape
    assert row_offset % tn == 0
    bps = blocks_per_step
    return pl.pallas_call(
        _proj_t_kernel,
        grid=(n // tn, m // (ts * bps)),
        in_specs=[pl.BlockSpec((tn, k), lambda a, i: (row_offset // tn + a, 0)),
                  pl.BlockSpec((ts * bps, k), lambda a, i: (i, 0)),
                  pl.BlockSpec((tn, 1), lambda a, i: (a, 0))],
        out_specs=pl.BlockSpec((bps, tn, ts), lambda a, i: (i, a, 0)),
        out_shape=jax.ShapeDtypeStruct((m // ts, n, ts), BF16),
        scratch_shapes=[pltpu.VMEM((tn, k), BF16)],
        compiler_params=_cparams(("parallel", "arbitrary"), 56),
        name=name,
    )(wt, x_bf, b_col)


_FIELD = N_HEADS


def _split3(c):
    hi = c.astype(BF16).astype(F32)
    r1 = c - hi
    mid = r1.astype(BF16).astype(F32)
    lo = (r1 - mid).astype(BF16).astype(F32)
    return hi, mid, lo


def _cum_kernel(x_ref, wf_ref, bf_ref, xbf_ref, kaug_ref, qall_ref, cfirst_ref, clast_ref, carry_ref, *, tb):
    @pl.when(pl.program_id(0) == 0)
    def _():
        carry_ref[...] = jnp.zeros_like(carry_ref)

    x_bf = x_ref[...].astype(BF16)
    xbf_ref[...] = x_bf
    f = lax.dot_general(x_bf, wf_ref[...], _NT, preferred_element_type=F32) + bf_ref[...]
    log_f = jnp.minimum(f, 0.0) - jnp.log1p(jnp.exp(-jnp.abs(f)))
    row = lax.broadcasted_iota(jnp.int32, (tb, tb), 0)
    col = lax.broadcasted_iota(jnp.int32, (tb, tb), 1)
    lower = (col <= row).astype(BF16)
    parts = jnp.concatenate(_split3(log_f), axis=1).astype(BF16)
    sums = jnp.dot(lower, parts, preferred_element_type=F32)
    cum = (sums[:, :LANES] + sums[:, LANES:2 * LANES]) + sums[:, 2 * LANES:] + carry_ref[...]
    carry_ref[...] = cum[tb - 1:tb, :]

    cum2 = cum * LOG2E
    cfirst_ref[0] = cum2[0:1, :]
    clast_ref[0] = cum2[tb - 1:tb, :]
    hi, mid, lo = _split3(cum2)
    lane = lax.broadcasted_iota(jnp.int32, (tb, LANES), 1)
    field = lambda v, n: pltpu.roll(v, n * _FIELD, axis=1)
    k_cols = jnp.where(lane < _FIELD, -hi,
             jnp.where(lane < 2 * _FIELD, -field(mid, 1),
             jnp.where(lane < 3 * _FIELD, -field(lo, 2),
             jnp.where(lane < 6 * _FIELD, 1.0, 0.0))))
    q_cols = jnp.where(lane < 3 * _FIELD, 1.0,
             jnp.where(lane < 4 * _FIELD, field(hi, 3),
             jnp.where(lane < 5 * _FIELD, field(mid, 4),
             jnp.where(lane < 6 * _FIELD, field(lo, 5), 0.0))))
    kaug_ref[...] = k_cols.astype(BF16)
    qall_ref[...] = q_cols.astype(BF16)


def _cum(x2, wf, bf, *, tb):
    m, k = x2.shape
    aug_shape = jax.ShapeDtypeStruct((m, LANES), BF16)
    aug_spec = pl.BlockSpec((tb, LANES), lambda i: (i, 0))
    edge_shape = jax.ShapeDtypeStruct((m // tb, 1, LANES), F32)
    edge_spec = pl.BlockSpec((1, 1, LANES), lambda i: (i, 0, 0))
    return pl.pallas_call(
        functools.partial(_cum_kernel, tb=tb),
        grid=(m // tb,),
        in_specs=[pl.BlockSpec((tb, k), lambda i: (i, 0)),
                  pl.BlockSpec((LANES, k), lambda i: (0, 0)),
                  pl.BlockSpec((1, LANES), lambda i: (0, 0))],
        out_specs=[pl.BlockSpec((tb, k), lambda i: (i, 0)), aug_spec, aug_spec, edge_spec, edge_spec],
        out_shape=[jax.ShapeDtypeStruct((m, k), BF16), aug_shape, aug_shape, edge_shape, edge_shape],
        scratch_shapes=[pltpu.VMEM((1, LANES), F32)],
        compiler_params=_cparams(("arbitrary",), 32),
        name="forget_cumsum",
    )(x2, wf, bf)


def _plan_kernel(qn2_ref, kn2_ref, cfirst_ref, clast_ref, o_ref, *, nq, q_per_norm_block, kv_per_q):
    kn = jnp.sqrt(jnp.max(kn2_ref[...], axis=0, keepdims=True))
    clast = clast_ref[...]
    n_kv = clast.shape[0]
    blk = lax.broadcasted_iota(jnp.int32, (n_kv, LANES), 0)
    assert kv_per_q == 2
    pair_shift = jnp.full((1, LANES), 1, jnp.int32)
    for i in range(nq):
        qn = jnp.sqrt(qn2_ref[i // q_per_norm_block:i // q_per_norm_block + 1, :])
        bound = (2.0 * NORM_INFLATE) * qn * kn + (cfirst_ref[kv_per_q * i:kv_per_q * i + 1, :] - clast)
        skip = (bound < -(F32_EXP2_UNDERFLOW + SKIP_SLACK)) & (blk < kv_per_q * i)
        first = jnp.min(jnp.where(skip, n_kv, blk), axis=0, keepdims=True)
        o_ref[i:i + 1, :] = lax.shift_right_logical(first, pair_shift)


def _plan(qn2, kn2, cfirst, clast, *, nq, q_per_norm_block, kv_per_q):
    full = lambda a: pl.BlockSpec(a.shape, lambda: (0,) * a.ndim)
    args = (qn2, kn2, cfirst, clast)
    return pl.pallas_call(
        functools.partial(_plan_kernel, nq=nq, q_per_norm_block=q_per_norm_block, kv_per_q=kv_per_q),
        in_specs=[full(a) for a in args],
        out_specs=pl.BlockSpec((nq, LANES), lambda: (0, 0)),
        out_shape=jax.ShapeDtypeStruct((nq, LANES), jnp.int32),
        name="attention_plan",
    )(*args)


def _attn_kernel(first_ref, q_ref, qall_ref, k_ref, kaug_ref, vt_ref, ga_ref, o_ref,
                 qa0_ref, qa1_ref, s0_ref, s1_ref, acc_ref, m_ref, l_ref, *, tq, tkv, nq):
    h = pl.program_id(0)
    lane = lax.broadcasted_iota(jnp.int32, (tq, LANES), 1)
    own = (lane & (_FIELD - 1)) == h

    def load_queries(i, qa_ref):
        r0 = pl.multiple_of(i * tq, tq)
        qall = qall_ref[pl.ds(r0, tq), :]
        qa_ref[...] = jnp.concatenate(
            [q_ref[pl.ds(r0, tq), :], jnp.where(own, qall, jnp.zeros_like(qall))], axis=1)

    def scores(j, qa_ref, s_ref, c0=0):
        r0 = pl.multiple_of(j * tkv, tkv)
        ka = jnp.concatenate([k_ref[pl.ds(r0, tkv), :], kaug_ref[pl.ds(r0, tkv), :]], axis=1)
        s_ref[:, c0:tq] = lax.dot_general(ka, qa_ref[c0:, :], _NT, preferred_element_type=F32)

    def softmax_pv(j, s_ref, c0=0, c1=tq, key_offset=None):
        s = s_ref[:, c0:c1]
        if key_offset is not None:
            key_pos = lax.broadcasted_iota(jnp.int32, s.shape, 0) + key_offset
            qry_pos = lax.broadcasted_iota(jnp.int32, s.shape, 1) + c0
            s = jnp.where(key_pos <= qry_pos, s, -jnp.inf)
        m_prev = m_ref[:, c0:c1]
        m_new = jnp.maximum(m_prev, jnp.max(s, axis=0, keepdims=True))
        alpha = jnp.exp2(m_prev - m_new)
        p = jnp.exp2(s - m_new)
        l_ref[:, c0:c1] = alpha * l_ref[:, c0:c1] + jnp.sum(p, axis=0, keepdims=True)
        pv = jnp.dot(vt_ref[j], p.astype(BF16), preferred_element_type=F32)
        acc_ref[:, c0:c1] = alpha * acc_ref[:, c0:c1] + pv
        m_ref[:, c0:c1] = m_new

    def first_pair(i):
        return first_ref[h * nq + i]

    def query_block(i, qa_ref, qa_next_ref):
        def pair(p, carry):
            j = 2 * p
            scores(j + 1, qa_ref, s1_ref)
            softmax_pv(j, s0_ref)
            scores(j + 2, qa_ref, s0_ref)
            softmax_pv(j + 1, s1_ref)
            return carry

        m_ref[...] = jnp.full_like(m_ref, -jnp.inf)
        l_ref[...] = jnp.zeros_like(l_ref)
        acc_ref[...] = jnp.zeros_like(acc_ref)
        lax.fori_loop(first_pair(i), i, pair, 0)
        scores(2 * i + 1, qa_ref, s1_ref, c0=tkv)
        softmax_pv(2 * i, s0_ref, 0, tkv, key_offset=0)
        softmax_pv(2 * i, s0_ref, tkv, tq)
        nxt = jnp.minimum(i + 1, nq - 1)
        load_queries(nxt, qa_next_ref)
        scores(2 * first_pair(nxt), qa_next_ref, s0_ref)
        softmax_pv(2 * i + 1, s1_ref, tkv, tq, key_offset=tkv)
        rows = pl.ds(pl.multiple_of(i * tq, tq), tq)
        o = (acc_ref[...] / l_ref[...]).T
        o_ref[rows, :] = (o * ga_ref[rows, :]).astype(o_ref.dtype)

    def two_query_blocks(t, carry):
        query_block(2 * t, qa0_ref, qa1_ref)
        query_block(2 * t + 1, qa1_ref, qa0_ref)
        return carry

    assert nq % 2 == 0
    load_queries(0, qa0_ref)
    scores(2 * first_pair(0), qa0_ref, s0_ref)
    lax.fori_loop(0, nq // 2, two_query_blocks, 0)


def _attn(first_pair, q, k, vt, qall, kaug, ga, *, tq, tkv):
    s = q.shape[0]
    assert tq == 2 * tkv and vt.shape == (s // tkv, W_ATTN, tkv)
    head = lambda h, f: (0, h)
    grid_spec = pltpu.PrefetchScalarGridSpec(
        num_scalar_prefetch=1,
        grid=(N_HEADS,),
        in_specs=[pl.BlockSpec((s, HEAD_DIM), head),
                  pl.BlockSpec((s, LANES), lambda h, f: (0, 0)),
                  pl.BlockSpec((s, HEAD_DIM), head),
                  pl.BlockSpec((s, LANES), lambda h, f: (0, 0)),
                  pl.BlockSpec((s // tkv, HEAD_DIM, tkv), lambda h, f: (0, h, 0)),
                  pl.BlockSpec((s, HEAD_DIM), head)],
        out_specs=pl.BlockSpec((s, HEAD_DIM), head),
        scratch_shapes=[pltpu.VMEM((tq, 2 * HEAD_DIM), BF16),
                        pltpu.VMEM((tq, 2 * HEAD_DIM), BF16),
                        pltpu.VMEM((tkv, tq + LANES), F32),
                        pltpu.VMEM((tkv, tq + LANES), F32),
                        pltpu.VMEM((HEAD_DIM, tq), F32),
                        pltpu.VMEM((1, tq), F32),
                        pltpu.VMEM((1, tq), F32)])
    return pl.pallas_call(
        functools.partial(_attn_kernel, tq=tq, tkv=tkv, nq=s // tq),
        grid_spec=grid_spec,
        out_shape=jax.ShapeDtypeStruct((s, W_ATTN), BF16),
        compiler_params=_cparams(("parallel",), 56),
        name="forgetting_attention",
    )(first_pair, q, qall, k, kaug, vt, ga)


_N_LC = W_CONV // LANES
_TAPS_PAD = 32


def _conv_kernel(ucur_ref, uprev_ref, gc_ref, wdw_ref, bdw_ref, g_ref, b_ref, wo_ref, o_ref, wobf_ref,
                 buf_ref, c_ref, *, tr, rc):
    i = pl.program_id(0)
    wobf_ref[...] = wo_ref[...].astype(BF16)
    lanes = lambda c: slice(c * LANES, (c + 1) * LANES)
    for c in range(_N_LC):
        buf_ref[c, 0:HALO, :] = jnp.where(i == 0, 0.0, uprev_ref[:, lanes(c)])
        buf_ref[c, HALO:HALO + tr, :] = ucur_ref[:, lanes(c)]
    lead = HALO - (CONV_WIDTH - 1)
    win_rows = rc + HALO

    def lane_chunk(c, carry):
        for r0 in range(0, tr, rc):
            win = buf_ref[c, r0:r0 + win_rows, :]
            acc = jnp.broadcast_to(bdw_ref[c], (rc, LANES))
            for b in range(SUBLANES):
                rot = win if b == 0 else pltpu.roll(win, win_rows - b, axis=0)
                for a in range(win_rows // SUBLANES):
                    j = SUBLANES * a + b - lead
                    if 0 <= j < CONV_WIDTH:
                        acc = acc + wdw_ref[c, j:j + 1, :] * rot[SUBLANES * a:SUBLANES * a + rc, :]
            c_ref[c, r0:r0 + rc, :] = acc
        return carry

    lax.fori_loop(0, _N_LC, lane_chunk, 0)

    total = c_ref[0]
    for c in range(1, _N_LC):
        total = total + c_ref[c]
    mu = jnp.sum(total, axis=-1, keepdims=True) * (1.0 / W_CONV)
    sq = jnp.zeros_like(total)
    for c in range(_N_LC):
        d = c_ref[c] - mu
        sq = sq + d * d
    rstd = lax.rsqrt(jnp.sum(sq, axis=-1, keepdims=True) * (1.0 / W_CONV) + LN_EPS)
    for c in range(_N_LC):
        y = (c_ref[c] - mu) * rstd * g_ref[:, lanes(c)] + b_ref[:, lanes(c)]
        y = y * _sigmoid(y)
        o_ref[:, lanes(c)] = (y * gc_ref[:, lanes(c)]).astype(o_ref.dtype)


def _conv(u, gc, w_dw, b_dw, g_cn, b_cn, w_out, *, tr):
    s, w = u.shape
    e, d = w_out.shape
    slab = e // (s // tr)
    assert slab * (s // tr) == e and slab % (2 * SUBLANES) == 0
    wdw = jnp.pad(w_dw, ((0, _TAPS_PAD - CONV_WIDTH), (0, 0))).reshape(_TAPS_PAD, _N_LC, LANES)
    wdw = wdw.transpose(1, 0, 2)
    bdw = b_dw.reshape(_N_LC, 1, LANES)
    vec = lambda: pl.BlockSpec((1, w), lambda i: (0, 0))
    return pl.pallas_call(
        functools.partial(_conv_kernel, tr=tr, rc=128),
        grid=(s // tr,),
        in_specs=[pl.BlockSpec((tr, w), lambda i: (i, 0)),
                  pl.BlockSpec((HALO, w), lambda i: (jnp.maximum(i * (tr // HALO) - 1, 0), 0)),
                  pl.BlockSpec((tr, w), lambda i: (i, 0)),
                  pl.BlockSpec((_N_LC, _TAPS_PAD, LANES), lambda i: (0, 0, 0)),
                  pl.BlockSpec((_N_LC, 1, LANES), lambda i: (0, 0, 0)),
                  vec(), vec(),
                  pl.BlockSpec((slab, d), lambda i: (i, 0))],
        out_specs=[pl.BlockSpec((tr, w), lambda i: (i, 0)),
                   pl.BlockSpec((slab, d), lambda i: (i, 0))],
        out_shape=[jax.ShapeDtypeStruct((s, w), BF16),
                   jax.ShapeDtypeStruct((e, d), BF16)],
        scratch_shapes=[pltpu.VMEM((_N_LC, HALO + tr, LANES), F32), pltpu.VMEM((_N_LC, tr, LANES), F32)],
        compiler_params=_cparams(("parallel",), 40),
        name="conformer_conv",
    )(u, u, gc, wdw, bdw, g_cn, b_cn, w_out)


def _out_kernel(yc_ref, ya_ref, w_ref, x_ref, bo_ref, g_ref, b_ref, o_ref, *, n_split):
    rows_per = o_ref.shape[0] // n_split
    for r in range(n_split):
        rows = slice(r * rows_per, (r + 1) * rows_per)
        sub = jnp.dot(yc_ref[rows, :], w_ref[0:W_CONV, :], preferred_element_type=F32)
        sub = sub + jnp.dot(ya_ref[rows, :], w_ref[W_CONV:W_CONV + W_ATTN, :], preferred_element_type=F32)
        h = DN_ALPHA * x_ref[rows, :] + (sub + bo_ref[...])
        o_ref[rows, :] = _layer_norm(h, g_ref[...], b_ref[...])


def _out(yc, ya, w_out_bf, x2, b_out, g_post, b_post, *, tm):
    s, d = x2.shape
    e = w_out_bf.shape[0]
    vec = lambda: pl.BlockSpec((1, d), lambda i: (0, 0))
    return pl.pallas_call(
        functools.partial(_out_kernel, n_split=2),
        grid=(s // tm,),
        in_specs=[pl.BlockSpec((tm, W_CONV), lambda i: (i, 0)),
                  pl.BlockSpec((tm, W_ATTN), lambda i: (i, 0)),
                  pl.BlockSpec((e, d), lambda i: (0, 0), pipeline_mode=pl.Buffered(1)),
                  pl.BlockSpec((tm, d), lambda i: (i, 0)),
                  vec(), vec(), vec()],
        out_specs=pl.BlockSpec((tm, d), lambda i: (i, 0)),
        out_shape=jax.ShapeDtypeStruct((s, d), F32),
        compiler_params=_cparams(("parallel",), 56),
        name="out_proj_deepnorm",
    )(yc, ya, w_out_bf, x2, b_out, g_post, b_post)


def kernel(x, w_in, b_in, w_dw, b_dw, g_conv_norm, b_conv_norm, w_out, b_out, g_post, b_post):
    assert x.shape == (1, SEQ, D_MODEL)
    x2 = x[0]
    row = lambda v: v.reshape(1, -1)
    bias = lambda a, n: row(b_in[a:a + n])
    lane_pad = lambda a: jnp.pad(a, ((0, 0), (0, LANES - a.shape[1])))

    o_val, o_gate, o_gc, o_q, o_k, o_v, o_f = 0, 2048, 4096, 6144, 8192, 10240, 12288
    o_ga = o_f + N_HEADS
    wt = w_in.T
    wt_f = jnp.pad(wt[o_f:o_ga], ((0, LANES - N_HEADS), (0, 0))).astype(BF16)
    b_f = lane_pad(bias(o_f, N_HEADS))

    tq, tkv = 1024, 512
    tm, tn = 2048, 512
    q_scale = LOG2E / math.sqrt(HEAD_DIM)
    silu = lambda z: z * _sigmoid(z)
    x_bf, kaug, qall, cfirst, clast = _cum(x2, wt_f, b_f, tb=tkv)
    u = _proj(x_bf, wt, [o_val, o_gate], [bias(o_val, W_CONV), bias(o_gate, W_CONV)], W_CONV,
              lambda a, g: a * _sigmoid(g), F32, tm=tm, tn=tn, name="proj_glu")
    gc = _proj(x_bf, wt, [o_gc], [bias(o_gc, W_CONV)], W_CONV, silu, F32,
               tm=tm, tn=tn, name="proj_gate_conv")
    q, qn2 = _proj(x_bf, wt, [o_q], [bias(o_q, W_ATTN)], W_ATTN, lambda z: z * q_scale, BF16,
                   tm=tm, tn=tn, name="proj_q", head_norms=True)
    k, kn2 = _proj(x_bf, wt, [o_k], [bias(o_k, W_ATTN)], W_ATTN, lambda z: z, BF16,
                   tm=tm, tn=tn, name="proj_k", head_norms=True)
    vt = _proj_t(x_bf, wt, o_v, b_in[o_v:o_v + W_ATTN].reshape(-1, 1), W_ATTN,
                 ts=tkv, tn=1024, blocks_per_step=4, name="proj_v_transposed")
    ga = _proj(x_bf, wt, [o_ga], [bias(o_ga, W_ATTN)], W_ATTN, silu, F32,
               tm=tm, tn=tn, name="proj_gate_attn")

    first_pair = _plan(lane_pad(qn2), lane_pad(kn2), cfirst[:, 0], clast[:, 0],
                       nq=SEQ // tq, q_per_norm_block=tm // tq, kv_per_q=tq // tkv)
    first_pair = first_pair[:, :N_HEADS].T.reshape(-1)

    y_conv, w_out_bf = _conv(u, gc, w_dw, row(b_dw), row(g_conv_norm), row(b_conv_norm), w_out, tr=256)
    y_attn = _attn(first_pair, q, k, vt, qall, kaug, ga, tq=tq, tkv=tkv)
    out = _out(y_conv, y_attn, w_out_bf, x2, row(b_out), row(g_post), row(b_post), tm=512)
    return out[None]
```

```python
import functools
import math

import jax
import jax.numpy as jnp
from jax import lax
from jax.experimental import pallas as pl
from jax.experimental.pallas import tpu as pltpu

D_MODEL = 2048
SEQ = 8192
W_CONV = 2048
W_ATTN = 2048
HEAD_DIM = 128
N_HEADS = W_ATTN // HEAD_DIM
CONV_WIDTH = 31
LN_EPS = 1e-5
DN_ALPHA = 2.0 ** 0.25
LOG2E = math.log2(math.e)
LANES = 128
SUBLANES = 8
HALO = 32
MIB = 1024 * 1024

F32_EXP2_UNDERFLOW = 152.0
SKIP_SLACK = 2.0
NORM_INFLATE = 1.02

F32 = jnp.float32
BF16 = jnp.bfloat16

_NT = (((1,), (1,)), ((), ()))


def _cparams(semantics, vmem_mib):
    return pltpu.CompilerParams(dimension_semantics=semantics,
                                vmem_limit_bytes=vmem_mib * MIB)


def _sigmoid(x):
    return 1.0 / (1.0 + jnp.exp(-x))


def _layer_norm(h, g, b):
    mu = jnp.mean(h, axis=-1, keepdims=True)
    d = h - mu
    var = jnp.mean(d * d, axis=-1, keepdims=True)
    return d * lax.rsqrt(var + LN_EPS) * g + b


def _proj_kernel(x_ref, *refs, n_w, epilogue, head_norms, n_split):
    w_refs, b_refs, o_ref = refs[:n_w], refs[n_w:2 * n_w], refs[2 * n_w]
    ws = [w[...].astype(BF16) for w in w_refs]
    rows_per = o_ref.shape[0] // n_split
    for r in range(n_split):
        rows = slice(r * rows_per, (r + 1) * rows_per)
        zs = [lax.dot_general(x_ref[rows, :], w, _NT, preferred_element_type=F32) + b[...]
              for w, b in zip(ws, b_refs)]
        o_ref[rows, :] = epilogue(*zs).astype(o_ref.dtype)
    if head_norms:
        y = o_ref[...]
        nrm_ref = refs[2 * n_w + 1]
        sq = y.astype(F32)
        sq = sq * sq
        lane = lax.broadcasted_iota(jnp.int32, (1, LANES), 1)
        out = jnp.zeros((1, LANES), F32)
        for hh in range(y.shape[1] // HEAD_DIM):
            n2 = jnp.sum(sq[:, hh * HEAD_DIM:(hh + 1) * HEAD_DIM], axis=-1, keepdims=True)
            out = jnp.where(lane == hh, jnp.max(n2, axis=0, keepdims=True), out)
        nrm_ref[0] = out


def _proj(x_bf, wt, row_offsets, bs, n, epilogue, out_dtype, *, tm, tn, name, head_norms=False):
    m, k = x_bf.shape
    n_w = len(row_offsets)
    assert all(off % SUBLANES == 0 for off in row_offsets)

    def weight_spec(off):
        if off % tn == 0:
            return pl.BlockSpec((tn, k), lambda i, j: (off // tn + j, 0))
        return pl.BlockSpec((pl.Element(tn), pl.Element(k)),
                            lambda i, j: (pl.multiple_of(off + j * tn, SUBLANES), 0))

    in_specs = [pl.BlockSpec((tm, k), lambda i, j: (i, 0))]
    in_specs += [weight_spec(off) for off in row_offsets]
    in_specs += [pl.BlockSpec((1, tn), lambda i, j: (0, j)) for _ in bs]
    out_specs = [pl.BlockSpec((tm, tn), lambda i, j: (i, j))]
    out_shape = [jax.ShapeDtypeStruct((m, n), out_dtype)]
    nj = n // tn
    if head_norms:
        out_specs.append(pl.BlockSpec((1, 1, LANES), lambda i, j: (i * nj + j, 0, 0)))
        out_shape.append(jax.ShapeDtypeStruct((m // tm * nj, 1, LANES), F32))
    res = pl.pallas_call(
        functools.partial(_proj_kernel, n_w=n_w, epilogue=epilogue, head_norms=head_norms, n_split=tm // 256),
        grid=(m // tm, nj), in_specs=in_specs, out_specs=out_specs, out_shape=out_shape,
        compiler_params=_cparams(("parallel", "arbitrary"), 56),
        name=name,
    )(x_bf, *([wt] * n_w), *bs)
    if not head_norms:
        return res[0]
    heads_per_tile = tn // HEAD_DIM
    norms = res[1].reshape(m // tm, nj, LANES)[:, :, :heads_per_tile].reshape(m // tm, n // HEAD_DIM)
    return res[0], norms


def _proj_t_kernel(w_ref, x_ref, b_ref, o_ref, wbf_ref):
    @pl.when(pl.program_id(1) == 0)
    def _():
        wbf_ref[...] = w_ref[...].astype(BF16)

    ts = o_ref.shape[2]
    for r in range(o_ref.shape[0]):
        z = lax.dot_general(wbf_ref[...], x_ref[r * ts:(r + 1) * ts, :], _NT,
                            preferred_element_type=F32)
        o_ref[r] = (z + b_ref[...]).astype(o_ref.dtype)


def _proj_t(x_bf, wt, row_offset, b_col, n, *, ts, tn, blocks_per_step, name):
    m, k = x_bf.shape
    assert row_offset % tn == 0
    bps = blocks_per_step
    return pl.pallas_call(
        _proj_t_kernel,
        grid=(n // tn, m // (ts * bps)),
        in_specs=[pl.BlockSpec((tn, k), lambda a, i: (row_offset // tn + a, 0)),
                  pl.BlockSpec((ts * bps, k), lambda a, i: (i, 0)),
                  pl.BlockSpec((tn, 1), lambda a, i: (a, 0))],
        out_specs=pl.BlockSpec((bps, tn, ts), lambda a, i: (i, a, 0)),
        out_shape=jax.ShapeDtypeStruct((m // ts, n, ts), BF16),
        scratch_shapes=[pltpu.VMEM((tn, k), BF16)],
        compiler_params=_cparams(("parallel", "arbitrary"), 56),
        name=name,
    )(wt, x_bf, b_col)


_FIELD = N_HEADS


def _split3(c):
    hi = c.astype(BF16).astype(F32)
    r1 = c - hi
    mid = r1.astype(BF16).astype(F32)
    lo = (r1 - mid).astype(BF16).astype(F32)
    return hi, mid, lo


def _cum_kernel(x_ref, wf_ref, bf_ref, xbf_ref, kaug_ref, qall_ref, cfirst_ref, clast_ref, carry_ref, *, tb):
    @pl.when(pl.program_id(0) == 0)
    def _():
        carry_ref[...] = jnp.zeros_like(carry_ref)

    x_bf = x_ref[...].astype(BF16)
    xbf_ref[...] = x_bf
    f = lax.dot_general(x_bf, wf_ref[...], _NT, preferred_element_type=F32) + bf_ref[...]
    log_f = jnp.minimum(f, 0.0) - jnp.log1p(jnp.exp(-jnp.abs(f)))
    row = lax.broadcasted_iota(jnp.int32, (tb, tb), 0)
    col = lax.broadcasted_iota(jnp.int32, (tb, tb), 1)
    lower = (col <= row).astype(BF16)
    parts = jnp.concatenate(_split3(log_f), axis=1).astype(BF16)
    sums = jnp.dot(lower, parts, preferred_element_type=F32)
    cum = (sums[:, :LANES] + sums[:, LANES:2 * LANES]) + sums[:, 2 * LANES:] + carry_ref[...]
    carry_ref[...] = cum[tb - 1:tb, :]

    cum2 = cum * LOG2E
    cfirst_ref[0] = cum2[0:1, :]
    clast_ref[0] = cum2[tb - 1:tb, :]
    hi, mid, lo = _split3(cum2)
    lane = lax.broadcasted_iota(jnp.int32, (tb, LANES), 1)
    field = lambda v, n: pltpu.roll(v, n * _FIELD, axis=1)
    k_cols = jnp.where(lane < _FIELD, -hi,
             jnp.where(lane < 2 * _FIELD, -field(mid, 1),
             jnp.where(lane < 3 * _FIELD, -field(lo, 2),
             jnp.where(lane < 6 * _FIELD, 1.0, 0.0))))
    q_cols = jnp.where(lane < 3 * _FIELD, 1.0,
             jnp.where(lane < 4 * _FIELD, field(hi, 3),
             jnp.where(lane < 5 * _FIELD, field(mid, 4),
             jnp.where(lane < 6 * _FIELD, field(lo, 5), 0.0))))
    kaug_ref[...] = k_cols.astype(BF16)
    qall_ref[...] = q_cols.astype(BF16)


def _cum(x2, wf, bf, *, tb):
    m, k = x2.shape
    aug_shape = jax.ShapeDtypeStruct((m, LANES), BF16)
    aug_spec = pl.BlockSpec((tb, LANES), lambda i: (i, 0))
    edge_shape = jax.ShapeDtypeStruct((m // tb, 1, LANES), F32)
    edge_spec = pl.BlockSpec((1, 1, LANES), lambda i: (i, 0, 0))
    return pl.pallas_call(
        functools.partial(_cum_kernel, tb=tb),
        grid=(m // tb,),
        in_specs=[pl.BlockSpec((tb, k), lambda i: (i, 0)),
                  pl.BlockSpec((LANES, k), lambda i: (0, 0)),
                  pl.BlockSpec((1, LANES), lambda i: (0, 0))],
        out_specs=[pl.BlockSpec((tb, k), lambda i: (i, 0)), aug_spec, aug_spec, edge_spec, edge_spec],
        out_shape=[jax.ShapeDtypeStruct((m, k), BF16), aug_shape, aug_shape, edge_shape, edge_shape],
        scratch_shapes=[pltpu.VMEM((1, LANES), F32)],
        compiler_params=_cparams(("arbitrary",), 32),
        name="forget_cumsum",
    )(x2, wf, bf)


def _plan_kernel(qn2_ref, kn2_ref, cfirst_ref, clast_ref, o_ref, *, nq, q_per_norm_block, kv_per_q):
    kn = jnp.sqrt(jnp.max(kn2_ref[...], axis=0, keepdims=True))
    clast = clast_ref[...]
    n_kv = clast.shape[0]
    blk = lax.broadcasted_iota(jnp.int32, (n_kv, LANES), 0)
    assert kv_per_q == 2
    pair_shift = jnp.full((1, LANES), 1, jnp.int32)
    for i in range(nq):
        qn = jnp.sqrt(qn2_ref[i // q_per_norm_block:i // q_per_norm_block + 1, :])
        bound = (2.0 * NORM_INFLATE) * qn * kn + (cfirst_ref[kv_per_q * i:kv_per_q * i + 1, :] - clast)
        skip = (bound < -(F32_EXP2_UNDERFLOW + SKIP_SLACK)) & (blk < kv_per_q * i)
        first = jnp.min(jnp.where(skip, n_kv, blk), axis=0, keepdims=True)
        o_ref[i:i + 1, :] = lax.shift_right_logical(first, pair_shift)


def _plan(qn2, kn2, cfirst, clast, *, nq, q_per_norm_block, kv_per_q):
    full = lambda a: pl.BlockSpec(a.shape, lambda: (0,) * a.ndim)
    args = (qn2, kn2, cfirst, clast)
    return pl.pallas_call(
        functools.partial(_plan_kernel, nq=nq, q_per_norm_block=q_per_norm_block, kv_per_q=kv_per_q),
        in_specs=[full(a) for a in args],
        out_specs=pl.BlockSpec((nq, LANES), lambda: (0, 0)),
        out_shape=jax.ShapeDtypeStruct((nq, LANES), jnp.int32),
        name="attention_plan",
    )(*args)


def _attn_kernel(first_ref, q_ref, qall_ref, k_ref, kaug_ref, vt_ref, ga_ref, o_ref,
                 qa0_ref, qa1_ref, s0_ref, s1_ref, s2_ref, acc_ref, m_ref, l_ref, *, tq, tkv, nq):
    h = pl.program_id(0)
    lane = lax.broadcasted_iota(jnp.int32, (tq, LANES), 1)
    own = (lane & (_FIELD - 1)) == h

    def load_queries(i, qa_ref):
        r0 = pl.multiple_of(i * tq, tq)
        qall = qall_ref[pl.ds(r0, tq), :]
        qa_ref[...] = jnp.concatenate(
            [q_ref[pl.ds(r0, tq), :], jnp.where(own, qall, jnp.zeros_like(qall))], axis=1)

    def scores(j, qa_ref, s_ref, c0=0):
        r0 = pl.multiple_of(j * tkv, tkv)
        ka = jnp.concatenate([k_ref[pl.ds(r0, tkv), :], kaug_ref[pl.ds(r0, tkv), :]], axis=1)
        s_ref[:, c0:] = lax.dot_general(ka, qa_ref[c0:, :], _NT, preferred_element_type=F32)

    def softmax_pv(j, s_ref, c0=0, c1=tq, key_offset=None):
        s = s_ref[:, c0:c1]
        if key_offset is not None:
            key_pos = lax.broadcasted_iota(jnp.int32, s.shape, 0) + key_offset
            qry_pos = lax.broadcasted_iota(jnp.int32, s.shape, 1) + c0
            s = jnp.where(key_pos <= qry_pos, s, -jnp.inf)
        m_prev = m_ref[:, c0:c1]
        m_new = jnp.maximum(m_prev, jnp.max(s, axis=0, keepdims=True))
        alpha = jnp.exp2(m_prev - m_new)
        p = jnp.exp2(s - m_new)
        l_ref[:, c0:c1] = alpha * l_ref[:, c0:c1] + jnp.sum(p, axis=0, keepdims=True)
        pv = jnp.dot(vt_ref[j], p.astype(BF16), preferred_element_type=F32)
        acc_ref[:, c0:c1] = alpha * acc_ref[:, c0:c1] + pv
        m_ref[:, c0:c1] = m_new

    def first_pair(i):
        return first_ref[h * nq + i]

    def query_block(i, qa_ref, qa_next_ref, sf_ref, sf_next_ref):
        def pair(p, carry):
            j = 2 * p
            scores(j + 1, qa_ref, s1_ref)
            softmax_pv(j, sf_ref)
            scores(j + 2, qa_ref, sf_ref)
            softmax_pv(j + 1, s1_ref)
            return carry

        m_ref[...] = jnp.full_like(m_ref, -jnp.inf)
        l_ref[...] = jnp.zeros_like(l_ref)
        acc_ref[...] = jnp.zeros_like(acc_ref)
        lax.fori_loop(first_pair(i), i, pair, 0)
        scores(2 * i + 1, qa_ref, s1_ref, c0=tkv)
        nxt = jnp.minimum(i + 1, nq - 1)
        load_queries(nxt, qa_next_ref)
        scores(2 * first_pair(nxt), qa_next_ref, sf_next_ref)
        softmax_pv(2 * i, sf_ref, 0, tkv, key_offset=0)
        softmax_pv(2 * i, sf_ref, tkv, tq)
        softmax_pv(2 * i + 1, s1_ref, tkv, tq, key_offset=tkv)
        rows = pl.ds(pl.multiple_of(i * tq, tq), tq)
        o = (acc_ref[...] / l_ref[...]).T
        o_ref[rows, :] = (o * ga_ref[rows, :]).astype(o_ref.dtype)

    def two_query_blocks(t, carry):
        query_block(2 * t, qa0_ref, qa1_ref, s0_ref, s2_ref)
        query_block(2 * t + 1, qa1_ref, qa0_ref, s2_ref, s0_ref)
        return carry

    assert nq % 2 == 0
    load_queries(0, qa0_ref)
    scores(2 * first_pair(0), qa0_ref, s0_ref)
    lax.fori_loop(0, nq // 2, two_query_blocks, 0)


def _attn(first_pair, q, k, vt, qall, kaug, ga, *, tq, tkv):
    s = q.shape[0]
    assert tq == 2 * tkv and vt.shape == (s // tkv, W_ATTN, tkv)
    head = lambda h, f: (0, h)
    grid_spec = pltpu.PrefetchScalarGridSpec(
        num_scalar_prefetch=1,
        grid=(N_HEADS,),
        in_specs=[pl.BlockSpec((s, HEAD_DIM), head),
                  pl.BlockSpec((s, LANES), lambda h, f: (0, 0)),
                  pl.BlockSpec((s, HEAD_DIM), head),
                  pl.BlockSpec((s, LANES), lambda h, f: (0, 0)),
                  pl.BlockSpec((s // tkv, HEAD_DIM, tkv), lambda h, f: (0, h, 0)),
                  pl.BlockSpec((s, HEAD_DIM), head)],
        out_specs=pl.BlockSpec((s, HEAD_DIM), head),
        scratch_shapes=[pltpu.VMEM((tq, 2 * HEAD_DIM), BF16),
                        pltpu.VMEM((tq, 2 * HEAD_DIM), BF16),
                        pltpu.VMEM((tkv, tq), F32),
                        pltpu.VMEM((tkv, tq), F32),
                        pltpu.VMEM((tkv, tq), F32),
                        pltpu.VMEM((HEAD_DIM, tq), F32),
                        pltpu.VMEM((1, tq), F32),
                        pltpu.VMEM((1, tq), F32)])
    return pl.pallas_call(
        functools.partial(_attn_kernel, tq=tq, tkv=tkv, nq=s // tq),
        grid_spec=grid_spec,
        out_shape=jax.ShapeDtypeStruct((s, W_ATTN), BF16),
        compiler_params=_cparams(("parallel",), 56),
        name="forgetting_attention",
    )(first_pair, q, qall, k, kaug, vt, ga)


_N_LC = W_CONV // LANES
_TAPS_PAD = 32


def _conv_kernel(ucur_ref, uprev_ref, gc_ref, wdw_ref, bdw_ref, g_ref, b_ref, wo_ref, o_ref, wobf_ref,
                 buf_ref, c_ref, *, tr, rc):
    i = pl.program_id(0)
    wobf_ref[...] = wo_ref[...].astype(BF16)
    lanes = lambda c: slice(c * LANES, (c + 1) * LANES)
    for c in range(_N_LC):
        buf_ref[c, 0:HALO, :] = jnp.where(i == 0, 0.0, uprev_ref[:, lanes(c)])
        buf_ref[c, HALO:HALO + tr, :] = ucur_ref[:, lanes(c)]
    lead = HALO - (CONV_WIDTH - 1)
    win_rows = rc + HALO

    def lane_chunk(c, carry):
        for r0 in range(0, tr, rc):
            win = buf_ref[c, r0:r0 + win_rows, :]
            acc = jnp.broadcast_to(bdw_ref[c], (rc, LANES))
            for b in range(SUBLANES):
                rot = win if b == 0 else pltpu.roll(win, win_rows - b, axis=0)
                for a in range(win_rows // SUBLANES):
                    j = SUBLANES * a + b - lead
                    if 0 <= j < CONV_WIDTH:
                        acc = acc + wdw_ref[c, j:j + 1, :] * rot[SUBLANES * a:SUBLANES * a + rc, :]
            c_ref[c, r0:r0 + rc, :] = acc
        return carry

    lax.fori_loop(0, _N_LC, lane_chunk, 0)

    total = c_ref[0]
    for c in range(1, _N_LC):
        total = total + c_ref[c]
    mu = jnp.sum(total, axis=-1, keepdims=True) * (1.0 / W_CONV)
    sq = jnp.zeros_like(total)
    for c in range(_N_LC):
        d = c_ref[c] - mu
        sq = sq + d * d
    rstd = lax.rsqrt(jnp.sum(sq, axis=-1, keepdims=True) * (1.0 / W_CONV) + LN_EPS)
    for c in range(_N_LC):
        y = (c_ref[c] - mu) * rstd * g_ref[:, lanes(c)] + b_ref[:, lanes(c)]
        y = y * _sigmoid(y)
        o_ref[:, lanes(c)] = (y * gc_ref[:, lanes(c)]).astype(o_ref.dtype)


def _conv(u, gc, w_dw, b_dw, g_cn, b_cn, w_out, *, tr):
    s, w = u.shape
    e, d = w_out.shape
    slab = e // (s // tr)
    assert slab * (s // tr) == e and slab % (2 * SUBLANES) == 0
    wdw = jnp.pad(w_dw, ((0, _TAPS_PAD - CONV_WIDTH), (0, 0))).reshape(_TAPS_PAD, _N_LC, LANES)
    wdw = wdw.transpose(1, 0, 2)
    bdw = b_dw.reshape(_N_LC, 1, LANES)
    vec = lambda: pl.BlockSpec((1, w), lambda i: (0, 0))
    return pl.pallas_call(
        functools.partial(_conv_kernel, tr=tr, rc=128),
        grid=(s // tr,),
        in_specs=[pl.BlockSpec((tr, w), lambda i: (i, 0)),
                  pl.BlockSpec((HALO, w), lambda i: (jnp.maximum(i * (tr // HALO) - 1, 0), 0)),
                  pl.BlockSpec((tr, w), lambda i: (i, 0)),
                  pl.BlockSpec((_N_LC, _TAPS_PAD, LANES), lambda i: (0, 0, 0)),
                  pl.BlockSpec((_N_LC, 1, LANES), lambda i: (0, 0, 0)),
                  vec(), vec(),
                  pl.BlockSpec((slab, d), lambda i: (i, 0))],
        out_specs=[pl.BlockSpec((tr, w), lambda i: (i, 0)),
                   pl.BlockSpec((slab, d), lambda i: (i, 0))],
        out_shape=[jax.ShapeDtypeStruct((s, w), BF16),
                   jax.ShapeDtypeStruct((e, d), BF16)],
        scratch_shapes=[pltpu.VMEM((_N_LC, HALO + tr, LANES), F32), pltpu.VMEM((_N_LC, tr, LANES), F32)],
        compiler_params=_cparams(("parallel",), 40),
        name="conformer_conv",
    )(u, u, gc, wdw, bdw, g_cn, b_cn, w_out)


def _out_kernel(yc_ref, ya_ref, w_ref, x_ref, bo_ref, g_ref, b_ref, o_ref, *, n_split):
    rows_per = o_ref.shape[0] // n_split
    for r in range(n_split):
        rows = slice(r * rows_per, (r + 1) * rows_per)
        sub = jnp.dot(yc_ref[rows, :], w_ref[0:W_CONV, :], preferred_element_type=F32)
        sub = sub + jnp.dot(ya_ref[rows, :], w_ref[W_CONV:W_CONV + W_ATTN, :], preferred_element_type=F32)
        h = DN_ALPHA * x_ref[rows, :] + (sub + bo_ref[...])
        o_ref[rows, :] = _layer_norm(h, g_ref[...], b_ref[...])


def _out(yc, ya, w_out_bf, x2, b_out, g_post, b_post, *, tm):
    s, d = x2.shape
    e = w_out_bf.shape[0]
    vec = lambda: pl.BlockSpec((1, d), lambda i: (0, 0))
    return pl.pallas_call(
        functools.partial(_out_kernel, n_split=2),
        grid=(s // tm,),
        in_specs=[pl.BlockSpec((tm, W_CONV), lambda i: (i, 0)),
                  pl.BlockSpec((tm, W_ATTN), lambda i: (i, 0)),
                  pl.BlockSpec((e, d), lambda i: (0, 0), pipeline_mode=pl.Buffered(1)),
                  pl.BlockSpec((tm, d), lambda i: (i, 0)),
                  vec(), vec(), vec()],
        out_specs=pl.BlockSpec((tm, d), lambda i: (i, 0)),
        out_shape=jax.ShapeDtypeStruct((s, d), F32),
        compiler_params=_cparams(("parallel",), 56),
        name="out_proj_deepnorm",
    )(yc, ya, w_out_bf, x2, b_out, g_post, b_post)


def kernel(x, w_in, b_in, w_dw, b_dw, g_conv_norm, b_conv_norm, w_out, b_out, g_post, b_post):
    assert x.shape == (1, SEQ, D_MODEL)
    x2 = x[0]
    row = lambda v: v.reshape(1, -1)
    bias = lambda a, n: row(b_in[a:a + n])
    lane_pad = lambda a: jnp.pad(a, ((0, 0), (0, LANES - a.shape[1])))

    o_val, o_gate, o_gc, o_q, o_k, o_v, o_f = 0, 2048, 4096, 6144, 8192, 10240, 12288
    o_ga = o_f + N_HEADS
    wt = w_in.T
    wt_f = jnp.pad(wt[o_f:o_ga], ((0, LANES - N_HEADS), (0, 0))).astype(BF16)
    b_f = lane_pad(bias(o_f, N_HEADS))

    tq, tkv = 1024, 512
    tm, tn = 2048, 512
    q_scale = LOG2E / math.sqrt(HEAD_DIM)
    silu = lambda z: z * _sigmoid(z)
    x_bf, kaug, qall, cfirst, clast = _cum(x2, wt_f, b_f, tb=tkv)
    u = _proj(x_bf, wt, [o_val, o_gate], [bias(o_val, W_CONV), bias(o_gate, W_CONV)], W_CONV,
              lambda a, g: a * _sigmoid(g), F32, tm=tm, tn=tn, name="proj_glu")
    gc = _proj(x_bf, wt, [o_gc], [bias(o_gc, W_CONV)], W_CONV, silu, F32,
               tm=tm, tn=tn, name="proj_gate_conv")
    q, qn2 = _proj(x_bf, wt, [o_q], [bias(o_q, W_ATTN)], W_ATTN, lambda z: z * q_scale, BF16,
                   tm=tm, tn=tn, name="proj_q", head_norms=True)
    k, kn2 = _proj(x_bf, wt, [o_k], [bias(o_k, W_ATTN)], W_ATTN, lambda z: z, BF16,
                   tm=tm, tn=tn, name="proj_k", head_norms=True)
    vt = _proj_t(x_bf, wt, o_v, b_in[o_v:o_v + W_ATTN].reshape(-1, 1), W_ATTN,
                 ts=tkv, tn=1024, blocks_per_step=4, name="proj_v_transposed")
    ga = _proj(x_bf, wt, [o_ga], [bias(o_ga, W_ATTN)], W_ATTN, silu, F32,
               tm=tm, tn=tn, name="proj_gate_attn")

    first_pair = _plan(lane_pad(qn2), lane_pad(kn2), cfirst[:, 0], clast[:, 0],
                       nq=SEQ // tq, q_per_norm_block=tm // tq, kv_per_q=tq // tkv)
    first_pair = first_pair[:, :N_HEADS].T.reshape(-1)

    y_conv, w_out_bf = _conv(u, gc, w_dw, row(b_dw), row(g_conv_norm), row(b_conv_norm), w_out, tr=256)
    y_attn = _attn(first_pair, q, k, vt, qall, kaug, ga, tq=tq, tkv=tkv)
    out = _out(y_conv, y_attn, w_out_bf, x2, row(b_out), row(g_post), row(b_post), tm=512)
    return out[None]
```

```python
import functools
import math

import jax
import jax.numpy as jnp
from jax import lax
from jax.experimental import pallas as pl
from jax.experimental.pallas import tpu as pltpu

D_MODEL = 2048
SEQ = 8192
W_CONV = 2048
W_ATTN = 2048
HEAD_DIM = 128
N_HEADS = W_ATTN // HEAD_DIM
CONV_WIDTH = 31
LN_EPS = 1e-5
DN_ALPHA = 2.0 ** 0.25
LOG2E = math.log2(math.e)
LANES = 128
SUBLANES = 8
HALO = 32
MIB = 1024 * 1024

F32_EXP2_UNDERFLOW = 152.0
SKIP_SLACK = 2.0
NORM_INFLATE = 1.02

F32 = jnp.float32
BF16 = jnp.bfloat16

_NT = (((1,), (1,)), ((), ()))


def _cparams(semantics, vmem_mib):
    return pltpu.CompilerParams(dimension_semantics=semantics,
                                vmem_limit_bytes=vmem_mib * MIB)


def _sigmoid(x):
    return 1.0 / (1.0 + jnp.exp(-x))


def _layer_norm(h, g, b):
    mu = jnp.mean(h, axis=-1, keepdims=True)
    d = h - mu
    var = jnp.mean(d * d, axis=-1, keepdims=True)
    return d * lax.rsqrt(var + LN_EPS) * g + b


def _proj_kernel(x_ref, *refs, n_w, epilogue, head_norms, n_split, chunk_major):
    w_refs, b_refs, o_ref = refs[:n_w], refs[n_w:2 * n_w], refs[2 * n_w]
    ws = [w[...].astype(BF16) for w in w_refs]
    rows_per = x_ref.shape[0] // n_split
    for r in range(n_split):
        rows = slice(r * rows_per, (r + 1) * rows_per)
        zs = [lax.dot_general(x_ref[rows, :], w, _NT, preferred_element_type=F32) + b[...]
              for w, b in zip(ws, b_refs)]
        y = epilogue(*zs).astype(o_ref.dtype)
        if chunk_major:
            for cc in range(o_ref.shape[0]):
                o_ref[cc, rows, :] = y[:, cc * LANES:(cc + 1) * LANES]
        else:
            o_ref[rows, :] = y
    if head_norms:
        y = o_ref[...]
        nrm_ref = refs[2 * n_w + 1]
        sq = y.astype(F32)
        sq = sq * sq
        lane = lax.broadcasted_iota(jnp.int32, (1, LANES), 1)
        out = jnp.zeros((1, LANES), F32)
        for hh in range(y.shape[1] // HEAD_DIM):
            n2 = jnp.sum(sq[:, hh * HEAD_DIM:(hh + 1) * HEAD_DIM], axis=-1, keepdims=True)
            out = jnp.where(lane == hh, jnp.max(n2, axis=0, keepdims=True), out)
        nrm_ref[0] = out


def _proj(x_bf, wt, row_offsets, bs, n, epilogue, out_dtype, *, tm, tn, name, head_norms=False,
          chunk_major=False):
    m, k = x_bf.shape
    n_w = len(row_offsets)
    assert all(off % SUBLANES == 0 for off in row_offsets)

    def weight_spec(off):
        if off % tn == 0:
            return pl.BlockSpec((tn, k), lambda i, j: (off // tn + j, 0))
        return pl.BlockSpec((pl.Element(tn), pl.Element(k)),
                            lambda i, j: (pl.multiple_of(off + j * tn, SUBLANES), 0))

    in_specs = [pl.BlockSpec((tm, k), lambda i, j: (i, 0))]
    in_specs += [weight_spec(off) for off in row_offsets]
    in_specs += [pl.BlockSpec((1, tn), lambda i, j: (0, j)) for _ in bs]
    if chunk_major:
        out_specs = [pl.BlockSpec((tn // LANES, tm, LANES), lambda i, j: (j, i, 0))]
        out_shape = [jax.ShapeDtypeStruct((n // LANES, m, LANES), out_dtype)]
    else:
        out_specs = [pl.BlockSpec((tm, tn), lambda i, j: (i, j))]
        out_shape = [jax.ShapeDtypeStruct((m, n), out_dtype)]
    nj = n // tn
    if head_norms:
        out_specs.append(pl.BlockSpec((1, 1, LANES), lambda i, j: (i * nj + j, 0, 0)))
        out_shape.append(jax.ShapeDtypeStruct((m // tm * nj, 1, LANES), F32))
    res = pl.pallas_call(
        functools.partial(_proj_kernel, n_w=n_w, epilogue=epilogue, head_norms=head_norms, n_split=tm // 256,
                          chunk_major=chunk_major),
        grid=(m // tm, nj), in_specs=in_specs, out_specs=out_specs, out_shape=out_shape,
        compiler_params=_cparams(("parallel", "arbitrary"), 56),
        name=name,
    )(x_bf, *([wt] * n_w), *bs)
    if not head_norms:
        return res[0]
    heads_per_tile = tn // HEAD_DIM
    norms = res[1].reshape(m // tm, nj, LANES)[:, :, :heads_per_tile].reshape(m // tm, n // HEAD_DIM)
    return res[0], norms


def _proj_t_kernel(w_ref, x_ref, b_ref, o_ref, wbf_ref):
    @pl.when(pl.program_id(1) == 0)
    def _():
        wbf_ref[...] = w_ref[...].astype(BF16)

    ts = o_ref.shape[2]
    for r in range(o_ref.shape[0]):
        z = lax.dot_general(wbf_ref[...], x_ref[r * ts:(r + 1) * ts, :], _NT,
                            preferred_element_type=F32)
        o_ref[r] = (z + b_ref[...]).astype(o_ref.dtype)


def _proj_t(x_bf, wt, row_offset, b_col, n, *, ts, tn, blocks_per_step, name):
    m, k = x_bf.shape
    assert row_offset % tn == 0
    bps = blocks_per_step
    return pl.pallas_call(
        _proj_t_kernel,
        grid=(n // tn, m // (ts * bps)),
        in_specs=[pl.BlockSpec((tn, k), lambda a, i: (row_offset // tn + a, 0)),
                  pl.BlockSpec((ts * bps, k), lambda a, i: (i, 0)),
                  pl.BlockSpec((tn, 1), lambda a, i: (a, 0))],
        out_specs=pl.BlockSpec((bps, tn, ts), lambda a, i: (i, a, 0)),
        out_shape=jax.ShapeDtypeStruct((m // ts, n, ts), BF16),
        scratch_shapes=[pltpu.VMEM((tn, k), BF16)],
        compiler_params=_cparams(("parallel", "arbitrary"), 56),
        name=name,
    )(wt, x_bf, b_col)


_FIELD = N_HEADS


def _split3(c):
    hi = c.astype(BF16).astype(F32)
    r1 = c - hi
    mid = r1.astype(BF16).astype(F32)
    lo = (r1 - mid).astype(BF16).astype(F32)
    return hi, mid, lo


def _cum_kernel(x_ref, wf_ref, bf_ref, xbf_ref, kaug_ref, qall_ref, cfirst_ref, clast_ref, carry_ref, *, tb):
    @pl.when(pl.program_id(0) == 0)
    def _():
        carry_ref[...] = jnp.zeros_like(carry_ref)

    x_bf = x_ref[...].astype(BF16)
    xbf_ref[...] = x_bf
    f = lax.dot_general(x_bf, wf_ref[...], _NT, preferred_element_type=F32) + bf_ref[...]
    log_f = jnp.minimum(f, 0.0) - jnp.log1p(jnp.exp(-jnp.abs(f)))
    row = lax.broadcasted_iota(jnp.int32, (tb, tb), 0)
    col = lax.broadcasted_iota(jnp.int32, (tb, tb), 1)
    lower = (col <= row).astype(BF16)
    parts = jnp.concatenate(_split3(log_f), axis=1).astype(BF16)
    sums = jnp.dot(lower, parts, preferred_element_type=F32)
    cum = (sums[:, :LANES] + sums[:, LANES:2 * LANES]) + sums[:, 2 * LANES:] + carry_ref[...]
    carry_ref[...] = cum[tb - 1:tb, :]

    cum2 = cum * LOG2E
    cfirst_ref[0] = cum2[0:1, :]
    clast_ref[0] = cum2[tb - 1:tb, :]
    hi, mid, lo = _split3(cum2)
    lane = lax.broadcasted_iota(jnp.int32, (tb, LANES), 1)
    field = lambda v, n: pltpu.roll(v, n * _FIELD, axis=1)
    k_cols = jnp.where(lane < _FIELD, -hi,
             jnp.where(lane < 2 * _FIELD, -field(mid, 1),
             jnp.where(lane < 3 * _FIELD, -field(lo, 2),
             jnp.where(lane < 6 * _FIELD, 1.0, 0.0))))
    q_cols = jnp.where(lane < 3 * _FIELD, 1.0,
             jnp.where(lane < 4 * _FIELD, field(hi, 3),
             jnp.where(lane < 5 * _FIELD, field(mid, 4),
             jnp.where(lane < 6 * _FIELD, field(lo, 5), 0.0))))
    kaug_ref[...] = k_cols.astype(BF16)
    qall_ref[...] = q_cols.astype(BF16)


def _cum(x2, wf, bf, *, tb):
    m, k = x2.shape
    aug_shape = jax.ShapeDtypeStruct((m, LANES), BF16)
    aug_spec = pl.BlockSpec((tb, LANES), lambda i: (i, 0))
    edge_shape = jax.ShapeDtypeStruct((m // tb, 1, LANES), F32)
    edge_spec = pl.BlockSpec((1, 1, LANES), lambda i: (i, 0, 0))
    return pl.pallas_call(
        functools.partial(_cum_kernel, tb=tb),
        grid=(m // tb,),
        in_specs=[pl.BlockSpec((tb, k), lambda i: (i, 0)),
                  pl.BlockSpec((LANES, k), lambda i: (0, 0)),
                  pl.BlockSpec((1, LANES), lambda i: (0, 0))],
        out_specs=[pl.BlockSpec((tb, k), lambda i: (i, 0)), aug_spec, aug_spec, edge_spec, edge_spec],
        out_shape=[jax.ShapeDtypeStruct((m, k), BF16), aug_shape, aug_shape, edge_shape, edge_shape],
        scratch_shapes=[pltpu.VMEM((1, LANES), F32)],
        compiler_params=_cparams(("arbitrary",), 32),
        name="forget_cumsum",
    )(x2, wf, bf)


def _plan_kernel(qn2_ref, kn2_ref, cfirst_ref, clast_ref, o_ref, *, nq, q_per_norm_block, kv_per_q):
    kn = jnp.sqrt(jnp.max(kn2_ref[...], axis=0, keepdims=True))
    clast = clast_ref[...]
    n_kv = clast.shape[0]
    blk = lax.broadcasted_iota(jnp.int32, (n_kv, LANES), 0)
    assert kv_per_q == 2
    pair_shift = jnp.full((1, LANES), 1, jnp.int32)
    for i in range(nq):
        qn = jnp.sqrt(qn2_ref[i // q_per_norm_block:i // q_per_norm_block + 1, :])
        bound = (2.0 * NORM_INFLATE) * qn * kn + (cfirst_ref[kv_per_q * i:kv_per_q * i + 1, :] - clast)
        skip = (bound < -(F32_EXP2_UNDERFLOW + SKIP_SLACK)) & (blk < kv_per_q * i)
        first = jnp.min(jnp.where(skip, n_kv, blk), axis=0, keepdims=True)
        o_ref[i:i + 1, :] = lax.shift_right_logical(first, pair_shift)


def _plan(qn2, kn2, cfirst, clast, *, nq, q_per_norm_block, kv_per_q):
    full = lambda a: pl.BlockSpec(a.shape, lambda: (0,) * a.ndim)
    args = (qn2, kn2, cfirst, clast)
    return pl.pallas_call(
        functools.partial(_plan_kernel, nq=nq, q_per_norm_block=q_per_norm_block, kv_per_q=kv_per_q),
        in_specs=[full(a) for a in args],
        out_specs=pl.BlockSpec((nq, LANES), lambda: (0, 0)),
        out_shape=jax.ShapeDtypeStruct((nq, LANES), jnp.int32),
        name="attention_plan",
    )(*args)


def _attn_kernel(first_ref, q_ref, qall_ref, k_ref, kaug_ref, vt_ref, ga_ref, o_ref,
                 qa0_ref, qa1_ref, s0_ref, s1_ref, s2_ref, acc_ref, m_ref, l_ref, *, tq, tkv, nq):
    h = pl.program_id(0)
    lane = lax.broadcasted_iota(jnp.int32, (tq, LANES), 1)
    own = (lane & (_FIELD - 1)) == h

    def load_queries(i, qa_ref):
        r0 = pl.multiple_of(i * tq, tq)
        qall = qall_ref[pl.ds(r0, tq), :]
        qa_ref[...] = jnp.concatenate(
            [q_ref[pl.ds(r0, tq), :], jnp.where(own, qall, jnp.zeros_like(qall))], axis=1)

    def scores(j, qa_ref, s_ref, c0=0):
        r0 = pl.multiple_of(j * tkv, tkv)
        ka = jnp.concatenate([k_ref[pl.ds(r0, tkv), :], kaug_ref[pl.ds(r0, tkv), :]], axis=1)
        s_ref[:, c0:] = lax.dot_general(ka, qa_ref[c0:, :], _NT, preferred_element_type=F32)

    def softmax_pv(j, s_ref, c0=0, c1=tq, key_offset=None):
        s = s_ref[:, c0:c1]
        if key_offset is not None:
            key_pos = lax.broadcasted_iota(jnp.int32, s.shape, 0) + key_offset
            qry_pos = lax.broadcasted_iota(jnp.int32, s.shape, 1) + c0
            s = jnp.where(key_pos <= qry_pos, s, -jnp.inf)
        m_prev = m_ref[:, c0:c1]
        m_new = jnp.maximum(m_prev, jnp.max(s, axis=0, keepdims=True))
        alpha = jnp.exp2(m_prev - m_new)
        p = jnp.exp2(s - m_new)
        l_ref[:, c0:c1] = alpha * l_ref[:, c0:c1] + jnp.sum(p, axis=0, keepdims=True)
        pv = jnp.dot(vt_ref[j], p.astype(BF16), preferred_element_type=F32)
        acc_ref[:, c0:c1] = alpha * acc_ref[:, c0:c1] + pv
        m_ref[:, c0:c1] = m_new

    def first_pair(i):
        return first_ref[h * nq + i]

    def query_block(i, qa_ref, qa_next_ref, sf_ref, sf_next_ref):
        def pair(p, carry):
            j = 2 * p
            scores(j + 1, qa_ref, s1_ref)
            softmax_pv(j, sf_ref)
            scores(j + 2, qa_ref, sf_ref)
            softmax_pv(j + 1, s1_ref)
            return carry

        m_ref[...] = jnp.full_like(m_ref, -jnp.inf)
        l_ref[...] = jnp.zeros_like(l_ref)
        acc_ref[...] = jnp.zeros_like(acc_ref)
        lax.fori_loop(first_pair(i), i, pair, 0)
        scores(2 * i + 1, qa_ref, s1_ref, c0=tkv)
        nxt = jnp.minimum(i + 1, nq - 1)
        load_queries(nxt, qa_next_ref)
        scores(2 * first_pair(nxt), qa_next_ref, sf_next_ref)
        softmax_pv(2 * i, sf_ref, 0, tkv, key_offset=0)
        softmax_pv(2 * i, sf_ref, tkv, tq)
        softmax_pv(2 * i + 1, s1_ref, tkv, tq, key_offset=tkv)
        rows = pl.ds(pl.multiple_of(i * tq, tq), tq)
        o = (acc_ref[...] / l_ref[...]).T
        o_ref[rows, :] = (o * ga_ref[rows, :]).astype(o_ref.dtype)

    def two_query_blocks(t, carry):
        query_block(2 * t, qa0_ref, qa1_ref, s0_ref, s2_ref)
        query_block(2 * t + 1, qa1_ref, qa0_ref, s2_ref, s0_ref)
        return carry

    assert nq % 2 == 0
    load_queries(0, qa0_ref)
    scores(2 * first_pair(0), qa0_ref, s0_ref)
    lax.fori_loop(0, nq // 2, two_query_blocks, 0)


def _attn(first_pair, q, k, vt, qall, kaug, ga, *, tq, tkv):
    s = q.shape[0]
    assert tq == 2 * tkv and vt.shape == (s // tkv, W_ATTN, tkv)
    head = lambda h, f: (0, h)
    grid_spec = pltpu.PrefetchScalarGridSpec(
        num_scalar_prefetch=1,
        grid=(N_HEADS,),
        in_specs=[pl.BlockSpec((s, HEAD_DIM), head),
                  pl.BlockSpec((s, LANES), lambda h, f: (0, 0)),
                  pl.BlockSpec((s, HEAD_DIM), head),
                  pl.BlockSpec((s, LANES), lambda h, f: (0, 0)),
                  pl.BlockSpec((s // tkv, HEAD_DIM, tkv), lambda h, f: (0, h, 0)),
                  pl.BlockSpec((s, HEAD_DIM), head)],
        out_specs=pl.BlockSpec((s, HEAD_DIM), head),
        scratch_shapes=[pltpu.VMEM((tq, 2 * HEAD_DIM), BF16),
                        pltpu.VMEM((tq, 2 * HEAD_DIM), BF16),
                        pltpu.VMEM((tkv, tq), F32),
                        pltpu.VMEM((tkv, tq), F32),
                        pltpu.VMEM((tkv, tq), F32),
                        pltpu.VMEM((HEAD_DIM, tq), F32),
                        pltpu.VMEM((1, tq), F32),
                        pltpu.VMEM((1, tq), F32)])
    return pl.pallas_call(
        functools.partial(_attn_kernel, tq=tq, tkv=tkv, nq=s // tq),
        grid_spec=grid_spec,
        out_shape=jax.ShapeDtypeStruct((s, W_ATTN), BF16),
        compiler_params=_cparams(("parallel",), 56),
        name="forgetting_attention",
    )(first_pair, q, qall, k, kaug, vt, ga)


_N_LC = W_CONV // LANES
_TAPS_PAD = 32


def _conv_kernel(ucur_ref, uprev_ref, gc_ref, wdw_ref, bdw_ref, g_ref, b_ref, wo_ref, o_ref, wobf_ref,
                 c_ref, *, tr, rc):
    i = pl.program_id(0)
    wobf_ref[...] = wo_ref[...].astype(BF16)
    lanes = lambda c: slice(c * LANES, (c + 1) * LANES)
    lead = HALO - (CONV_WIDTH - 1)
    win_rows = rc + HALO

    def lane_chunk(c, carry):
        for r0 in range(0, tr, rc):
            if r0 == 0:
                history = jnp.where(i == 0, 0.0, uprev_ref[c])
                win = jnp.concatenate([history, ucur_ref[c, 0:rc, :]], axis=0)
            else:
                win = ucur_ref[c, r0 - HALO:r0 + rc, :]
            acc = jnp.broadcast_to(bdw_ref[c], (rc, LANES))
            for b in range(SUBLANES):
                rot = win if b == 0 else pltpu.roll(win, win_rows - b, axis=0)
                for a in range(win_rows // SUBLANES):
                    j = SUBLANES * a + b - lead
                    if 0 <= j < CONV_WIDTH:
                        acc = acc + wdw_ref[c, j:j + 1, :] * rot[SUBLANES * a:SUBLANES * a + rc, :]
            c_ref[c, r0:r0 + rc, :] = acc
        return carry

    lax.fori_loop(0, _N_LC, lane_chunk, 0)

    total = c_ref[0]
    for c in range(1, _N_LC):
        total = total + c_ref[c]
    mu = jnp.sum(total, axis=-1, keepdims=True) * (1.0 / W_CONV)
    sq = jnp.zeros_like(total)
    for c in range(_N_LC):
        d = c_ref[c] - mu
        sq = sq + d * d
    rstd = lax.rsqrt(jnp.sum(sq, axis=-1, keepdims=True) * (1.0 / W_CONV) + LN_EPS)
    for c in range(_N_LC):
        y = (c_ref[c] - mu) * rstd * g_ref[:, lanes(c)] + b_ref[:, lanes(c)]
        y = y * _sigmoid(y)
        o_ref[:, lanes(c)] = (y * gc_ref[:, lanes(c)]).astype(o_ref.dtype)


def _conv(u, gc, w_dw, b_dw, g_cn, b_cn, w_out, *, tr):
    s, w = gc.shape
    assert u.shape == (_N_LC, s, LANES)
    e, d = w_out.shape
    slab = e // (s // tr)
    assert slab * (s // tr) == e and slab % (2 * SUBLANES) == 0
    wdw = jnp.pad(w_dw, ((0, _TAPS_PAD - CONV_WIDTH), (0, 0))).reshape(_TAPS_PAD, _N_LC, LANES)
    wdw = wdw.transpose(1, 0, 2)
    bdw = b_dw.reshape(_N_LC, 1, LANES)
    vec = lambda: pl.BlockSpec((1, w), lambda i: (0, 0))
    return pl.pallas_call(
        functools.partial(_conv_kernel, tr=tr, rc=128),
        grid=(s // tr,),
        in_specs=[pl.BlockSpec((_N_LC, tr, LANES), lambda i: (0, i, 0)),
                  pl.BlockSpec((_N_LC, HALO, LANES),
                               lambda i: (0, jnp.maximum(i * (tr // HALO) - 1, 0), 0)),
                  pl.BlockSpec((tr, w), lambda i: (i, 0)),
                  pl.BlockSpec((_N_LC, _TAPS_PAD, LANES), lambda i: (0, 0, 0)),
                  pl.BlockSpec((_N_LC, 1, LANES), lambda i: (0, 0, 0)),
                  vec(), vec(),
                  pl.BlockSpec((slab, d), lambda i: (i, 0))],
        out_specs=[pl.BlockSpec((tr, w), lambda i: (i, 0)),
                   pl.BlockSpec((slab, d), lambda i: (i, 0))],
        out_shape=[jax.ShapeDtypeStruct((s, w), BF16),
                   jax.ShapeDtypeStruct((e, d), BF16)],
        scratch_shapes=[pltpu.VMEM((_N_LC, tr, LANES), F32)],
        compiler_params=_cparams(("parallel",), 40),
        name="conformer_conv",
    )(u, u, gc, wdw, bdw, g_cn, b_cn, w_out)


def _out_kernel(yc_ref, ya_ref, w_ref, x_ref, bo_ref, g_ref, b_ref, o_ref, *, n_split):
    rows_per = o_ref.shape[0] // n_split
    for r in range(n_split):
        rows = slice(r * rows_per, (r + 1) * rows_per)
        sub = jnp.dot(yc_ref[rows, :], w_ref[0:W_CONV, :], preferred_element_type=F32)
        sub = sub + jnp.dot(ya_ref[rows, :], w_ref[W_CONV:W_CONV + W_ATTN, :], preferred_element_type=F32)
        h = DN_ALPHA * x_ref[rows, :] + (sub + bo_ref[...])
        o_ref[rows, :] = _layer_norm(h, g_ref[...], b_ref[...])


def _out(yc, ya, w_out_bf, x2, b_out, g_post, b_post, *, tm):
    s, d = x2.shape
    e = w_out_bf.shape[0]
    vec = lambda: pl.BlockSpec((1, d), lambda i: (0, 0))
    return pl.pallas_call(
        functools.partial(_out_kernel, n_split=2),
        grid=(s // tm,),
        in_specs=[pl.BlockSpec((tm, W_CONV), lambda i: (i, 0)),
                  pl.BlockSpec((tm, W_ATTN), lambda i: (i, 0)),
                  pl.BlockSpec((e, d), lambda i: (0, 0), pipeline_mode=pl.Buffered(1)),
                  pl.BlockSpec((tm, d), lambda i: (i, 0)),
                  vec(), vec(), vec()],
        out_specs=pl.BlockSpec((tm, d), lambda i: (i, 0)),
        out_shape=jax.ShapeDtypeStruct((s, d), F32),
        compiler_params=_cparams(("parallel",), 56),
        name="out_proj_deepnorm",
    )(yc, ya, w_out_bf, x2, b_out, g_post, b_post)


def kernel(x, w_in, b_in, w_dw, b_dw, g_conv_norm, b_conv_norm, w_out, b_out, g_post, b_post):
    assert x.shape == (1, SEQ, D_MODEL)
    x2 = x[0]
    row = lambda v: v.reshape(1, -1)
    bias = lambda a, n: row(b_in[a:a + n])
    lane_pad = lambda a: jnp.pad(a, ((0, 0), (0, LANES - a.shape[1])))

    o_val, o_gate, o_gc, o_q, o_k, o_v, o_f = 0, 2048, 4096, 6144, 8192, 10240, 12288
    o_ga = o_f + N_HEADS
    wt = w_in.T
    wt_f = jnp.pad(wt[o_f:o_ga], ((0, LANES - N_HEADS), (0, 0))).astype(BF16)
    b_f = lane_pad(bias(o_f, N_HEADS))

    tq, tkv = 1024, 512
    tm, tn = 2048, 512
    q_scale = LOG2E / math.sqrt(HEAD_DIM)
    silu = lambda z: z * _sigmoid(z)
    x_bf, kaug, qall, cfirst, clast = _cum(x2, wt_f, b_f, tb=tkv)
    u = _proj(x_bf, wt, [o_val, o_gate], [bias(o_val, W_CONV), bias(o_gate, W_CONV)], W_CONV,
              lambda a, g: a * _sigmoid(g), F32, tm=tm, tn=tn, name="proj_glu", chunk_major=True)
    gc = _proj(x_bf, wt, [o_gc], [bias(o_gc, W_CONV)], W_CONV, silu, F32,
               tm=tm, tn=tn, name="proj_gate_conv")
    q, qn2 = _proj(x_bf, wt, [o_q], [bias(o_q, W_ATTN)], W_ATTN, lambda z: z * q_scale, BF16,
                   tm=tm, tn=tn, name="proj_q", head_norms=True)
    k, kn2 = _proj(x_bf, wt, [o_k], [bias(o_k, W_ATTN)], W_ATTN, lambda z: z, BF16,
                   tm=tm, tn=tn, name="proj_k", head_norms=True)
    vt = _proj_t(x_bf, wt, o_v, b_in[o_v:o_v + W_ATTN].reshape(-1, 1), W_ATTN,
                 ts=tkv, tn=1024, blocks_per_step=4, name="proj_v_transposed")
    ga = _proj(x_bf, wt, [o_ga], [bias(o_ga, W_ATTN)], W_ATTN, silu, F32,
               tm=tm, tn=tn, name="proj_gate_attn")

    first_pair = _plan(lane_pad(qn2), lane_pad(kn2), cfirst[:, 0], clast[:, 0],
                       nq=SEQ // tq, q_per_norm_block=tm // tq, kv_per_q=tq // tkv)
    first_pair = first_pair[:, :N_HEADS].T.reshape(-1)

    y_conv, w_out_bf = _conv(u, gc, w_dw, row(b_dw), row(g_conv_norm), row(b_conv_norm), w_out, tr=256)
    y_attn = _attn(first_pair, q, k, vt, qall, kaug, ga, tq=tq, tkv=tkv)
    out = _out(y_conv, y_attn, w_out_bf, x2, row(b_out), row(g_post), row(b_post), tm=512)
    return out[None]
```

```python
import functools
import math

import jax
import jax.numpy as jnp
from jax import lax
from jax.experimental import pallas as pl
from jax.experimental.pallas import tpu as pltpu

D_MODEL = 2048
SEQ = 8192
W_CONV = 2048
W_ATTN = 2048
HEAD_DIM = 128
N_HEADS = W_ATTN // HEAD_DIM
CONV_WIDTH = 31
LN_EPS = 1e-5
DN_ALPHA = 2.0 ** 0.25
LOG2E = math.log2(math.e)
LANES = 128
SUBLANES = 8
HALO = 32
MIB = 1024 * 1024

F32_EXP2_UNDERFLOW = 152.0
SKIP_SLACK = 2.0
NORM_INFLATE = 1.02

F32 = jnp.float32
BF16 = jnp.bfloat16

_NT = (((1,), (1,)), ((), ()))


def _cparams(semantics, vmem_mib):
    return pltpu.CompilerParams(dimension_semantics=semantics,
                                vmem_limit_bytes=vmem_mib * MIB)


def _sigmoid(x):
    return 1.0 / (1.0 + jnp.exp(-x))


def _layer_norm(h, g, b):
    mu = jnp.mean(h, axis=-1, keepdims=True)
    d = h - mu
    var = jnp.mean(d * d, axis=-1, keepdims=True)
    return d * lax.rsqrt(var + LN_EPS) * g + b


def _proj_kernel(x_ref, *refs, n_w, epilogue, head_norms, n_split, chunk_major):
    w_refs, b_refs, o_ref = refs[:n_w], refs[n_w:2 * n_w], refs[2 * n_w]
    ws = [w[...].astype(BF16) for w in w_refs]
    rows_per = x_ref.shape[0] // n_split
    for r in range(n_split):
        rows = slice(r * rows_per, (r + 1) * rows_per)
        zs = [lax.dot_general(x_ref[rows, :], w, _NT, preferred_element_type=F32) + b[...]
              for w, b in zip(ws, b_refs)]
        y = epilogue(*zs).astype(o_ref.dtype)
        if chunk_major:
            for cc in range(o_ref.shape[0]):
                o_ref[cc, rows, :] = y[:, cc * LANES:(cc + 1) * LANES]
        else:
            o_ref[rows, :] = y
    if head_norms:
        y = o_ref[...]
        nrm_ref = refs[2 * n_w + 1]
        sq = y.astype(F32)
        sq = sq * sq
        lane = lax.broadcasted_iota(jnp.int32, (1, LANES), 1)
        out = jnp.zeros((1, LANES), F32)
        for hh in range(y.shape[1] // HEAD_DIM):
            n2 = jnp.sum(sq[:, hh * HEAD_DIM:(hh + 1) * HEAD_DIM], axis=-1, keepdims=True)
            out = jnp.where(lane == hh, jnp.max(n2, axis=0, keepdims=True), out)
        nrm_ref[0] = out


def _proj(x_bf, wt, row_offsets, bs, n, epilogue, out_dtype, *, tm, tn, name, head_norms=False,
          chunk_major=False):
    m, k = x_bf.shape
    n_w = len(row_offsets)
    assert all(off % SUBLANES == 0 for off in row_offsets)

    def weight_spec(off):
        if off % tn == 0:
            return pl.BlockSpec((tn, k), lambda i, j: (off // tn + j, 0))
        return pl.BlockSpec((pl.Element(tn), pl.Element(k)),
                            lambda i, j: (pl.multiple_of(off + j * tn, SUBLANES), 0))

    in_specs = [pl.BlockSpec((tm, k), lambda i, j: (i, 0))]
    in_specs += [weight_spec(off) for off in row_offsets]
    in_specs += [pl.BlockSpec((1, tn), lambda i, j: (0, j)) for _ in bs]
    if chunk_major:
        out_specs = [pl.BlockSpec((tn // LANES, tm, LANES), lambda i, j: (j, i, 0))]
        out_shape = [jax.ShapeDtypeStruct((n // LANES, m, LANES), out_dtype)]
    else:
        out_specs = [pl.BlockSpec((tm, tn), lambda i, j: (i, j))]
        out_shape = [jax.ShapeDtypeStruct((m, n), out_dtype)]
    nj = n // tn
    if head_norms:
        out_specs.append(pl.BlockSpec((1, 1, LANES), lambda i, j: (i * nj + j, 0, 0)))
        out_shape.append(jax.ShapeDtypeStruct((m // tm * nj, 1, LANES), F32))
    res = pl.pallas_call(
        functools.partial(_proj_kernel, n_w=n_w, epilogue=epilogue, head_norms=head_norms, n_split=tm // 256,
                          chunk_major=chunk_major),
        grid=(m // tm, nj), in_specs=in_specs, out_specs=out_specs, out_shape=out_shape,
        compiler_params=_cparams(("parallel", "arbitrary"), 56),
        name=name,
    )(x_bf, *([wt] * n_w), *bs)
    if not head_norms:
        return res[0]
    heads_per_tile = tn // HEAD_DIM
    norms = res[1].reshape(m // tm, nj, LANES)[:, :, :heads_per_tile].reshape(m // tm, n // HEAD_DIM)
    return res[0], norms


def _proj_t_kernel(w_ref, x_ref, b_ref, o_ref, wbf_ref):
    @pl.when(pl.program_id(1) == 0)
    def _():
        wbf_ref[...] = w_ref[...].astype(BF16)

    ts = o_ref.shape[2]
    for r in range(o_ref.shape[0]):
        z = lax.dot_general(wbf_ref[...], x_ref[r * ts:(r + 1) * ts, :], _NT,
                            preferred_element_type=F32)
        o_ref[r] = (z + b_ref[...]).astype(o_ref.dtype)


def _proj_t(x_bf, wt, row_offset, b_col, n, *, ts, tn, blocks_per_step, name):
    m, k = x_bf.shape
    assert row_offset % tn == 0
    bps = blocks_per_step
    return pl.pallas_call(
        _proj_t_kernel,
        grid=(n // tn, m // (ts * bps)),
        in_specs=[pl.BlockSpec((tn, k), lambda a, i: (row_offset // tn + a, 0)),
                  pl.BlockSpec((ts * bps, k), lambda a, i: (i, 0)),
                  pl.BlockSpec((tn, 1), lambda a, i: (a, 0))],
        out_specs=pl.BlockSpec((bps, tn, ts), lambda a, i: (i, a, 0)),
        out_shape=jax.ShapeDtypeStruct((m // ts, n, ts), BF16),
        scratch_shapes=[pltpu.VMEM((tn, k), BF16)],
        compiler_params=_cparams(("parallel", "arbitrary"), 56),
        name=name,
    )(wt, x_bf, b_col)


_FIELD = N_HEADS


def _split3(c):
    hi = c.astype(BF16).astype(F32)
    r1 = c - hi
    mid = r1.astype(BF16).astype(F32)
    lo = (r1 - mid).astype(BF16).astype(F32)
    return hi, mid, lo


def _cum_kernel(x_ref, wf_ref, bf_ref, xbf_ref, kaug_ref, qall_ref, cfirst_ref, clast_ref, carry_ref, *, tb):
    @pl.when(pl.program_id(0) == 0)
    def _():
        carry_ref[...] = jnp.zeros_like(carry_ref)

    x_bf = x_ref[...].astype(BF16)
    xbf_ref[...] = x_bf
    f = lax.dot_general(x_bf, wf_ref[...], _NT, preferred_element_type=F32) + bf_ref[...]
    log_f = jnp.minimum(f, 0.0) - jnp.log1p(jnp.exp(-jnp.abs(f)))
    row = lax.broadcasted_iota(jnp.int32, (tb, tb), 0)
    col = lax.broadcasted_iota(jnp.int32, (tb, tb), 1)
    lower = (col <= row).astype(BF16)
    parts = jnp.concatenate(_split3(log_f), axis=1).astype(BF16)
    sums = jnp.dot(lower, parts, preferred_element_type=F32)
    cum = (sums[:, :LANES] + sums[:, LANES:2 * LANES]) + sums[:, 2 * LANES:] + carry_ref[...]
    carry_ref[...] = cum[tb - 1:tb, :]

    cum2 = cum * LOG2E
    cfirst_ref[0] = cum2[0:1, :]
    clast_ref[0] = cum2[tb - 1:tb, :]
    hi, mid, lo = _split3(cum2)
    lane = lax.broadcasted_iota(jnp.int32, (tb, LANES), 1)
    field = lambda v, n: pltpu.roll(v, n * _FIELD, axis=1)
    k_cols = jnp.where(lane < _FIELD, -hi,
             jnp.where(lane < 2 * _FIELD, -field(mid, 1),
             jnp.where(lane < 3 * _FIELD, -field(lo, 2),
             jnp.where(lane < 6 * _FIELD, 1.0, 0.0))))
    q_cols = jnp.where(lane < 3 * _FIELD, 1.0,
             jnp.where(lane < 4 * _FIELD, field(hi, 3),
             jnp.where(lane < 5 * _FIELD, field(mid, 4),
             jnp.where(lane < 6 * _FIELD, field(lo, 5), 0.0))))
    kaug_ref[...] = k_cols.astype(BF16)
    qall_ref[...] = q_cols.astype(BF16)


def _cum(x2, wf, bf, *, tb):
    m, k = x2.shape
    aug_shape = jax.ShapeDtypeStruct((m, LANES), BF16)
    aug_spec = pl.BlockSpec((tb, LANES), lambda i: (i, 0))
    edge_shape = jax.ShapeDtypeStruct((m // tb, 1, LANES), F32)
    edge_spec = pl.BlockSpec((1, 1, LANES), lambda i: (i, 0, 0))
    return pl.pallas_call(
        functools.partial(_cum_kernel, tb=tb),
        grid=(m // tb,),
        in_specs=[pl.BlockSpec((tb, k), lambda i: (i, 0)),
                  pl.BlockSpec((LANES, k), lambda i: (0, 0)),
                  pl.BlockSpec((1, LANES), lambda i: (0, 0))],
        out_specs=[pl.BlockSpec((tb, k), lambda i: (i, 0)), aug_spec, aug_spec, edge_spec, edge_spec],
        out_shape=[jax.ShapeDtypeStruct((m, k), BF16), aug_shape, aug_shape, edge_shape, edge_shape],
        scratch_shapes=[pltpu.VMEM((1, LANES), F32)],
        compiler_params=_cparams(("arbitrary",), 32),
        name="forget_cumsum",
    )(x2, wf, bf)


def _plan_kernel(qn2_ref, kn2_ref, cfirst_ref, clast_ref, o_ref, *, nq, q_per_norm_block, kv_per_q):
    kn = jnp.sqrt(jnp.max(kn2_ref[...], axis=0, keepdims=True))
    clast = clast_ref[...]
    n_kv = clast.shape[0]
    blk = lax.broadcasted_iota(jnp.int32, (n_kv, LANES), 0)
    assert kv_per_q == 2
    pair_shift = jnp.full((1, LANES), 1, jnp.int32)
    for i in range(nq):
        qn = jnp.sqrt(qn2_ref[i // q_per_norm_block:i // q_per_norm_block + 1, :])
        bound = (2.0 * NORM_INFLATE) * qn * kn + (cfirst_ref[kv_per_q * i:kv_per_q * i + 1, :] - clast)
        skip = (bound < -(F32_EXP2_UNDERFLOW + SKIP_SLACK)) & (blk < kv_per_q * i)
        first = jnp.min(jnp.where(skip, n_kv, blk), axis=0, keepdims=True)
        o_ref[i:i + 1, :] = lax.shift_right_logical(first, pair_shift)


def _plan(qn2, kn2, cfirst, clast, *, nq, q_per_norm_block, kv_per_q):
    full = lambda a: pl.BlockSpec(a.shape, lambda: (0,) * a.ndim)
    args = (qn2, kn2, cfirst, clast)
    return pl.pallas_call(
        functools.partial(_plan_kernel, nq=nq, q_per_norm_block=q_per_norm_block, kv_per_q=kv_per_q),
        in_specs=[full(a) for a in args],
        out_specs=pl.BlockSpec((nq, LANES), lambda: (0, 0)),
        out_shape=jax.ShapeDtypeStruct((nq, LANES), jnp.int32),
        name="attention_plan",
    )(*args)


def _attn_kernel(first_ref, q_ref, qall_ref, k_ref, kaug_ref, vt_ref, ga_ref, o_ref,
                 qa0_ref, qa1_ref, s0_ref, s1_ref, s2_ref, acc_ref, m_ref, l_ref, *, tq, tkv, nq):
    h = pl.program_id(0)
    lane = lax.broadcasted_iota(jnp.int32, (tq, LANES), 1)
    own = (lane & (_FIELD - 1)) == h

    def load_queries(i, qa_ref):
        r0 = pl.multiple_of(i * tq, tq)
        qall = qall_ref[pl.ds(r0, tq), :]
        qa_ref[...] = jnp.concatenate(
            [q_ref[pl.ds(r0, tq), :], jnp.where(own, qall, jnp.zeros_like(qall))], axis=1)

    def scores(j, qa_ref, s_ref, c0=0):
        r0 = pl.multiple_of(j * tkv, tkv)
        ka = jnp.concatenate([k_ref[pl.ds(r0, tkv), :], kaug_ref[pl.ds(r0, tkv), :]], axis=1)
        s_ref[:, c0:] = lax.dot_general(ka, qa_ref[c0:, :], _NT, preferred_element_type=F32)

    def softmax_pv(j, s_ref, c0=0, c1=tq, key_offset=None):
        s = s_ref[:, c0:c1]
        if key_offset is not None:
            key_pos = lax.broadcasted_iota(jnp.int32, s.shape, 0) + key_offset
            qry_pos = lax.broadcasted_iota(jnp.int32, s.shape, 1) + c0
            s = jnp.where(key_pos <= qry_pos, s, -jnp.inf)
        m_prev = m_ref[:, c0:c1]
        m_new = jnp.maximum(m_prev, jnp.max(s, axis=0, keepdims=True))
        alpha = jnp.exp2(m_prev - m_new)
        p = jnp.exp2(s - m_new)
        l_ref[:, c0:c1] = alpha * l_ref[:, c0:c1] + jnp.sum(p, axis=0, keepdims=True)
        pv = jnp.dot(vt_ref[j], p.astype(BF16), preferred_element_type=F32)
        acc_ref[:, c0:c1] = alpha * acc_ref[:, c0:c1] + pv
        m_ref[:, c0:c1] = m_new

    def first_pair(i):
        return first_ref[h * nq + i]

    def query_block(i, qa_ref, qa_next_ref, sf_ref, sf_next_ref):
        def pair(p, carry):
            j = 2 * p
            scores(j + 1, qa_ref, s1_ref)
            softmax_pv(j, sf_ref)
            scores(j + 2, qa_ref, sf_ref)
            softmax_pv(j + 1, s1_ref)
            return carry

        m_ref[...] = jnp.full_like(m_ref, -jnp.inf)
        l_ref[...] = jnp.zeros_like(l_ref)
        acc_ref[...] = jnp.zeros_like(acc_ref)
        lax.fori_loop(first_pair(i), i, pair, 0)
        scores(2 * i + 1, qa_ref, s1_ref, c0=tkv)
        nxt = jnp.minimum(i + 1, nq - 1)
        load_queries(nxt, qa_next_ref)
        scores(2 * first_pair(nxt), qa_next_ref, sf_next_ref)
        softmax_pv(2 * i, sf_ref, 0, tkv, key_offset=0)
        softmax_pv(2 * i, sf_ref, tkv, tq)
        softmax_pv(2 * i + 1, s1_ref, tkv, tq, key_offset=tkv)
        rows = pl.ds(pl.multiple_of(i * tq, tq), tq)
        o = (acc_ref[...] / l_ref[...]).T
        o_ref[rows, :] = (o * ga_ref[rows, :]).astype(o_ref.dtype)

    def two_query_blocks(t, carry):
        query_block(2 * t, qa0_ref, qa1_ref, s0_ref, s2_ref)
        query_block(2 * t + 1, qa1_ref, qa0_ref, s2_ref, s0_ref)
        return carry

    assert nq % 2 == 0
    load_queries(0, qa0_ref)
    scores(2 * first_pair(0), qa0_ref, s0_ref)
    lax.fori_loop(0, nq // 2, two_query_blocks, 0)


def _attn(first_pair, qk, vt, qall, kaug, ga, *, tq, tkv):
    s = qk.shape[0]
    assert tq == 2 * tkv and vt.shape == (s // tkv, W_ATTN, tkv) and qk.shape[1] == 2 * W_ATTN
    head = lambda h, f: (0, h)
    grid_spec = pltpu.PrefetchScalarGridSpec(
        num_scalar_prefetch=1,
        grid=(N_HEADS,),
        in_specs=[pl.BlockSpec((s, HEAD_DIM), head),
                  pl.BlockSpec((s, LANES), lambda h, f: (0, 0)),
                  pl.BlockSpec((s, HEAD_DIM), lambda h, f: (0, N_HEADS + h)),
                  pl.BlockSpec((s, LANES), lambda h, f: (0, 0)),
                  pl.BlockSpec((s // tkv, HEAD_DIM, tkv), lambda h, f: (0, h, 0)),
                  pl.BlockSpec((s, HEAD_DIM), head)],
        out_specs=pl.BlockSpec((s, HEAD_DIM), head),
        scratch_shapes=[pltpu.VMEM((tq, 2 * HEAD_DIM), BF16),
                        pltpu.VMEM((tq, 2 * HEAD_DIM), BF16),
                        pltpu.VMEM((tkv, tq), F32),
                        pltpu.VMEM((tkv, tq), F32),
                        pltpu.VMEM((tkv, tq), F32),
                        pltpu.VMEM((HEAD_DIM, tq), F32),
                        pltpu.VMEM((1, tq), F32),
                        pltpu.VMEM((1, tq), F32)])
    return pl.pallas_call(
        functools.partial(_attn_kernel, tq=tq, tkv=tkv, nq=s // tq),
        grid_spec=grid_spec,
        out_shape=jax.ShapeDtypeStruct((s, W_ATTN), BF16),
        compiler_params=_cparams(("parallel",), 56),
        name="forgetting_attention",
    )(first_pair, qk, qall, qk, kaug, vt, ga)


_N_LC = W_CONV // LANES
_TAPS_PAD = 32


def _conv_kernel(ucur_ref, uprev_ref, gc_ref, wdw_ref, bdw_ref, g_ref, b_ref, wo_ref, o_ref, wobf_ref,
                 c_ref, *, tr, rc):
    i = pl.program_id(0)
    wobf_ref[...] = wo_ref[...].astype(BF16)
    lanes = lambda c: slice(c * LANES, (c + 1) * LANES)
    lead = HALO - (CONV_WIDTH - 1)
    win_rows = rc + HALO

    def lane_chunk(c, carry):
        for r0 in range(0, tr, rc):
            if r0 == 0:
                history = jnp.where(i == 0, 0.0, uprev_ref[c])
                win = jnp.concatenate([history, ucur_ref[c, 0:rc, :]], axis=0)
            else:
                win = ucur_ref[c, r0 - HALO:r0 + rc, :]
            acc = jnp.broadcast_to(bdw_ref[c], (rc, LANES))
            for b in range(SUBLANES):
                rot = win if b == 0 else pltpu.roll(win, win_rows - b, axis=0)
                for a in range(win_rows // SUBLANES):
                    j = SUBLANES * a + b - lead
                    if 0 <= j < CONV_WIDTH:
                        acc = acc + wdw_ref[c, j:j + 1, :] * rot[SUBLANES * a:SUBLANES * a + rc, :]
            c_ref[c, r0:r0 + rc, :] = acc
        return carry

    lax.fori_loop(0, _N_LC, lane_chunk, 0)

    total = c_ref[0]
    for c in range(1, _N_LC):
        total = total + c_ref[c]
    mu = jnp.sum(total, axis=-1, keepdims=True) * (1.0 / W_CONV)
    sq = jnp.zeros_like(total)
    for c in range(_N_LC):
        d = c_ref[c] - mu
        sq = sq + d * d
    rstd = lax.rsqrt(jnp.sum(sq, axis=-1, keepdims=True) * (1.0 / W_CONV) + LN_EPS)
    for c in range(_N_LC):
        y = (c_ref[c] - mu) * rstd * g_ref[:, lanes(c)] + b_ref[:, lanes(c)]
        y = y * _sigmoid(y)
        o_ref[:, lanes(c)] = (y * gc_ref[:, lanes(c)]).astype(o_ref.dtype)


def _conv(u, gc, w_dw, b_dw, g_cn, b_cn, w_out, *, tr):
    s, w = gc.shape
    assert u.shape == (_N_LC, s, LANES)
    e, d = w_out.shape
    slab = e // (s // tr)
    assert slab * (s // tr) == e and slab % (2 * SUBLANES) == 0
    wdw = jnp.pad(w_dw, ((0, _TAPS_PAD - CONV_WIDTH), (0, 0))).reshape(_TAPS_PAD, _N_LC, LANES)
    wdw = wdw.transpose(1, 0, 2)
    bdw = b_dw.reshape(_N_LC, 1, LANES)
    vec = lambda: pl.BlockSpec((1, w), lambda i: (0, 0))
    return pl.pallas_call(
        functools.partial(_conv_kernel, tr=tr, rc=128),
        grid=(s // tr,),
        in_specs=[pl.BlockSpec((_N_LC, tr, LANES), lambda i: (0, i, 0)),
                  pl.BlockSpec((_N_LC, HALO, LANES),
                               lambda i: (0, jnp.maximum(i * (tr // HALO) - 1, 0), 0)),
                  pl.BlockSpec((tr, w), lambda i: (i, 0)),
                  pl.BlockSpec((_N_LC, _TAPS_PAD, LANES), lambda i: (0, 0, 0)),
                  pl.BlockSpec((_N_LC, 1, LANES), lambda i: (0, 0, 0)),
                  vec(), vec(),
                  pl.BlockSpec((slab, d), lambda i: (i, 0))],
        out_specs=[pl.BlockSpec((tr, w), lambda i: (i, 0)),
                   pl.BlockSpec((slab, d), lambda i: (i, 0))],
        out_shape=[jax.ShapeDtypeStruct((s, w), BF16),
                   jax.ShapeDtypeStruct((e, d), BF16)],
        scratch_shapes=[pltpu.VMEM((_N_LC, tr, LANES), F32)],
        compiler_params=_cparams(("parallel",), 40),
        name="conformer_conv",
    )(u, u, gc, wdw, bdw, g_cn, b_cn, w_out)


def _out_kernel(yc_ref, ya_ref, w_ref, x_ref, bo_ref, g_ref, b_ref, o_ref, *, n_split):
    rows_per = o_ref.shape[0] // n_split
    for r in range(n_split):
        rows = slice(r * rows_per, (r + 1) * rows_per)
        sub = jnp.dot(yc_ref[rows, :], w_ref[0:W_CONV, :], preferred_element_type=F32)
        sub = sub + jnp.dot(ya_ref[rows, :], w_ref[W_CONV:W_CONV + W_ATTN, :], preferred_element_type=F32)
        h = DN_ALPHA * x_ref[rows, :] + (sub + bo_ref[...])
        o_ref[rows, :] = _layer_norm(h, g_ref[...], b_ref[...])


def _out(yc, ya, w_out_bf, x2, b_out, g_post, b_post, *, tm):
    s, d = x2.shape
    e = w_out_bf.shape[0]
    vec = lambda: pl.BlockSpec((1, d), lambda i: (0, 0))
    return pl.pallas_call(
        functools.partial(_out_kernel, n_split=2),
        grid=(s // tm,),
        in_specs=[pl.BlockSpec((tm, W_CONV), lambda i: (i, 0)),
                  pl.BlockSpec((tm, W_ATTN), lambda i: (i, 0)),
                  pl.BlockSpec((e, d), lambda i: (0, 0), pipeline_mode=pl.Buffered(1)),
                  pl.BlockSpec((tm, d), lambda i: (i, 0)),
                  vec(), vec(), vec()],
        out_specs=pl.BlockSpec((tm, d), lambda i: (i, 0)),
        out_shape=jax.ShapeDtypeStruct((s, d), F32),
        compiler_params=_cparams(("parallel",), 56),
        name="out_proj_deepnorm",
    )(yc, ya, w_out_bf, x2, b_out, g_post, b_post)


def kernel(x, w_in, b_in, w_dw, b_dw, g_conv_norm, b_conv_norm, w_out, b_out, g_post, b_post):
    assert x.shape == (1, SEQ, D_MODEL)
    x2 = x[0]
    row = lambda v: v.reshape(1, -1)
    bias = lambda a, n: row(b_in[a:a + n])
    lane_pad = lambda a: jnp.pad(a, ((0, 0), (0, LANES - a.shape[1])))

    o_val, o_gate, o_gc, o_q, o_k, o_v, o_f = 0, 2048, 4096, 6144, 8192, 10240, 12288
    o_ga = o_f + N_HEADS
    wt = w_in.T
    wt_f = jnp.pad(wt[o_f:o_ga], ((0, LANES - N_HEADS), (0, 0))).astype(BF16)
    b_f = lane_pad(bias(o_f, N_HEADS))

    tq, tkv = 1024, 512
    tm, tn = 2048, 512
    q_scale = LOG2E / math.sqrt(HEAD_DIM)
    silu = lambda z: z * _sigmoid(z)
    x_bf, kaug, qall, cfirst, clast = _cum(x2, wt_f, b_f, tb=tkv)
    u = _proj(x_bf, wt, [o_val, o_gate], [bias(o_val, W_CONV), bias(o_gate, W_CONV)], W_CONV,
              lambda a, g: a * _sigmoid(g), F32, tm=tm, tn=tn, name="proj_glu", chunk_major=True)
    gc = _proj(x_bf, wt, [o_gc], [bias(o_gc, W_CONV)], W_CONV, silu, F32,
               tm=tm, tn=tn, name="proj_gate_conv")
    q_tiles = W_ATTN // tn
    qk, qkn2 = _proj(x_bf, wt, [o_q], [bias(o_q, 2 * W_ATTN)], 2 * W_ATTN,
                     lambda z: z * jnp.where(pl.program_id(1) < q_tiles, q_scale, 1.0), BF16,
                     tm=tm, tn=tn, name="proj_qk", head_norms=True)
    qn2, kn2 = qkn2[:, :N_HEADS], qkn2[:, N_HEADS:]
    vt = _proj_t(x_bf, wt, o_v, b_in[o_v:o_v + W_ATTN].reshape(-1, 1), W_ATTN,
                 ts=tkv, tn=1024, blocks_per_step=4, name="proj_v_transposed")
    ga = _proj(x_bf, wt, [o_ga], [bias(o_ga, W_ATTN)], W_ATTN, silu, F32,
               tm=tm, tn=tn, name="proj_gate_attn")

    first_pair = _plan(lane_pad(qn2), lane_pad(kn2), cfirst[:, 0], clast[:, 0],
                       nq=SEQ // tq, q_per_norm_block=tm // tq, kv_per_q=tq // tkv)
    first_pair = first_pair[:, :N_HEADS].T.reshape(-1)

    y_conv, w_out_bf = _conv(u, gc, w_dw, row(b_dw), row(g_conv_norm), row(b_conv_norm), w_out, tr=256)
    y_attn = _attn(first_pair, qk, vt, qall, kaug, ga, tq=tq, tkv=tkv)
    out = _out(y_conv, y_attn, w_out_bf, x2, row(b_out), row(g_post), row(b_post), tm=512)
    return out[None]
```
